```python
import jax, jax.numpy as jnp
from jax import lax
import numpy as np

D_MODEL = 1024
BATCH = 16
SEQ = 2048
DEPTH = 2

HEAD_DIM = 64
GROUP_WIDTH = D_MODEL // 4
N_GROUP_HEADS = GROUP_WIDTH // HEAD_DIM
D_MIX = 4 * GROUP_WIDTH

RWKV_DECAY_RANK = 32
RWKV_AAA_RANK = 32
RWKV_GATE_RANK = 64
RWKV_GN_EPS = 64e-5
RWKV_COLS = 3 * GROUP_WIDTH + RWKV_DECAY_RANK + RWKV_AAA_RANK + RWKV_GATE_RANK

LRU_CONV_WIDTH = 4
LRU_C = 8.0

MOBA_BLOCK = 256
MOBA_TOPK = 3
Q_BLOCK = 128
ROPE_THETA = 10000.0
NEG_INF = -1e30

RET_CHUNK = 128
GN_EPS = 1e-5

D_FF = 2816
FFN_CONV_WIDTH = 3
LN_EPS = 1e-5

DEEPNORM_ALPHA = (2.0 * DEPTH) ** 0.25
DEEPNORM_BETA = (8.0 * DEPTH) ** -0.25

IN_SIZES = (RWKV_COLS,
            GROUP_WIDTH, GROUP_WIDTH,
            GROUP_WIDTH, GROUP_WIDTH, GROUP_WIDTH,
            GROUP_WIDTH, GROUP_WIDTH, GROUP_WIDTH, GROUP_WIDTH)
IN_COLS = sum(IN_SIZES)
IN_SPLITS = tuple(int(s) for s in np.cumsum(IN_SIZES)[:-1])
RWKV_SIZES = (GROUP_WIDTH, GROUP_WIDTH, GROUP_WIDTH, RWKV_DECAY_RANK, RWKV_AAA_RANK, RWKV_GATE_RANK)
RWKV_SPLITS = tuple(int(s) for s in np.cumsum(RWKV_SIZES)[:-1])

kernel_name = 'hymba_style_rwkv7_rglru_moba_retnet_trunk'


def layer_norm(x, g, b):
    xf = x.astype(jnp.float32)
    mu = xf.mean(-1, keepdims=True)
    var = jnp.mean(jnp.square(xf - mu), -1, keepdims=True)
    return ((xf - mu) * lax.rsqrt(var + LN_EPS) * g + b).astype(x.dtype)


def head_norm(y, g, b, eps):
    yf = y.astype(jnp.float32)
    mu = yf.mean(-1, keepdims=True)
    var = jnp.mean(jnp.square(yf - mu), -1, keepdims=True)
    shape = y.shape[-2:]
    return (yf - mu) * lax.rsqrt(var + eps) * g.reshape(shape) + b.reshape(shape)


def causal_depthwise_conv(x, w, b):
    k_width, ch = w.shape
    y = lax.conv_general_dilated(x, w[:, None, :].astype(x.dtype), window_strides=(1,),
                                 padding=[(k_width - 1, 0)],
                                 dimension_numbers=('NWC', 'WIO', 'NWC'),
                                 feature_group_count=ch)
    return y + b


def token_shift(p):
    return jnp.pad(p[:, :-1], ((0, 0), (1, 0), (0, 0)))


def rope_tables(seq, dim):
    inv = ROPE_THETA ** (-jnp.arange(0, dim, 2, dtype=jnp.float32) / dim)
    ang = jnp.arange(seq, dtype=jnp.float32)[:, None] * inv[None, :]
    return jnp.cos(ang), jnp.sin(ang)


def apply_rope(x, cos, sin):
    x1, x2 = jnp.split(x.astype(jnp.float32), 2, axis=-1)
    return jnp.concatenate([x1 * cos - x2 * sin, x1 * sin + x2 * cos], axis=-1).astype(x.dtype)


def rwkv7_mix(pa, mu, w0, w_up, a0, a_up, g_up, k_k, k_a, r_k, gn_g, gn_b):
    bsz, seq, _ = pa.shape
    H, N = N_GROUP_HEADS, HEAD_DIM
    pa = pa + (token_shift(pa) - pa) * mu
    r, k, v, wd, ad, gd = jnp.split(pa, RWKV_SPLITS, axis=-1)
    w_log = -jax.nn.softplus(-(w0 + jnp.tanh(wd) @ w_up)) - 0.5
    decay = jnp.exp(-jnp.exp(w_log.astype(jnp.float32)))
    a = jax.nn.sigmoid(a0 + ad @ a_up)
    g = jax.nn.sigmoid(gd) @ g_up
    kk = k * k_k
    k = k * (1.0 + (a - 1.0) * k_a)
    heads = lambda t: t.reshape(bsz, seq, H, N).astype(jnp.float32)
    r, k, v, kk, decay, a = map(heads, (r, k, v, kk, decay, a))
    kk = kk / jnp.maximum(jnp.linalg.norm(kk, axis=-1, keepdims=True), 1e-12)
    b_vec = kk * a

    def step(state, inp):
        r_t, k_t, v_t, kk_t, b_t, w_t = inp
        sa = jnp.einsum('bhij,bhj->bhi', state, kk_t)
        state = (state * w_t[..., None, :] - sa[..., :, None] * b_t[..., None, :]
                 + v_t[..., :, None] * k_t[..., None, :])
        return state, jnp.einsum('bhij,bhj->bhi', state, r_t)

    xs = tuple(jnp.moveaxis(t, 1, 0) for t in (r, k, v, kk, b_vec, decay))
    _, y = lax.scan(step, jnp.zeros((bsz, H, N, N), jnp.float32), xs)
    y = head_norm(jnp.moveaxis(y, 0, 1), gn_g, gn_b, RWKV_GN_EPS)
    y = y + jnp.sum(r * k * r_k, axis=-1, keepdims=True) * v
    return (y.reshape(bsz, seq, H * N) * g).astype(pa.dtype)


def rglru_mix(xb, gb, conv_w, conv_b, w_r, b_r, w_i, b_i, lam):
    bsz, seq, ch = xb.shape
    H = N_GROUP_HEADS
    xc = causal_depthwise_conv(xb, conv_w, conv_b)
    xh = xc.reshape(bsz, seq, H, ch // H)
    gate_r = jax.nn.sigmoid(jnp.einsum('bshi,hij->bshj', xh, w_r).reshape(bsz, seq, ch) + b_r)
    gate_i = jax.nn.sigmoid(jnp.einsum('bshi,hij->bshj', xh, w_i).reshape(bsz, seq, ch) + b_i)
    log_a = (-LRU_C * gate_r * jax.nn.softplus(-lam)).astype(jnp.float32)
    a = jnp.exp(log_a)
    u = jnp.sqrt(-jnp.expm1(2.0 * log_a)) * (gate_i * xc).astype(jnp.float32)

    def combine(left, right):
        a_l, h_l = left
        a_r, h_r = right
        return a_l * a_r, a_r * h_l + h_r

    _, h = lax.associative_scan(combine, (a, u), axis=1)
    return (jax.nn.gelu(gb) * h).astype(xb.dtype)


def moba_attention(q, k, v, cos, sin):
    bsz, seq, ch = q.shape
    H, D = N_GROUP_HEADS, HEAD_DIM
    to_heads = lambda t: t.reshape(bsz, seq, H, D).transpose(0, 2, 1, 3)
    q = apply_rope(to_heads(q), cos, sin)
    k = apply_rope(to_heads(k), cos, sin)
    v = to_heads(v)
    n_blocks = -(-seq // MOBA_BLOCK)
    pad = n_blocks * MOBA_BLOCK - seq
    k_blocks = jnp.pad(k, ((0, 0), (0, 0), (0, pad), (0, 0))).reshape(bsz, H, n_blocks, MOBA_BLOCK, D)
    v_blocks = jnp.pad(v, ((0, 0), (0, 0), (0, pad), (0, 0))).reshape(bsz, H, n_blocks, MOBA_BLOCK, D)
    k_mean = k_blocks.astype(jnp.float32).mean(axis=3)
    top = min(MOBA_TOPK, n_blocks)
    scale = D ** -0.5
    b_idx = jnp.arange(bsz)[:, None, None]
    h_idx = jnp.arange(H)[None, :, None]

    def one_chunk(c):
        q0 = c * Q_BLOCK
        blk = q0 // MOBA_BLOCK
        qc = lax.dynamic_slice_in_dim(q, q0, Q_BLOCK, axis=2).astype(jnp.float32)
        q_pos = q0 + jnp.arange(Q_BLOCK)
        gate = jnp.einsum('bhqd,bhnd->bhqn', qc, k_mean)
        gate = jnp.where(jnp.arange(n_blocks) < blk, gate, -jnp.inf)
        _, sel = lax.top_k(gate, top)
        k_own = lax.dynamic_slice_in_dim(k_blocks, blk, 1, axis=2)[:, :, 0]
        v_own = lax.dynamic_slice_in_dim(v_blocks, blk, 1, axis=2)[:, :, 0]
        k_pos = blk * MOBA_BLOCK + jnp.arange(MOBA_BLOCK)
        s_own = jnp.einsum('bhqd,bhkd->bhqk', qc, k_own) * scale
        scores = [jnp.where(k_pos[None, :] <= q_pos[:, None], s_own, NEG_INF)]
        for s in range(top):
            kg = k_blocks[b_idx, h_idx, sel[..., s]]
            s_sel = jnp.einsum('bhqd,bhqkd->bhqk', qc, kg) * scale
            scores.append(jnp.where(s < blk, s_sel, NEG_INF))
        p = jax.nn.softmax(jnp.concatenate(scores, axis=-1), axis=-1)
        out = jnp.einsum('bhqk,bhkd->bhqd', p[..., :MOBA_BLOCK], v_own)
        for s in range(top):
            vg = v_blocks[b_idx, h_idx, sel[..., s]]
            out = out + jnp.einsum('bhqk,bhqkd->bhqd',
                                   p[..., (s + 1) * MOBA_BLOCK:(s + 2) * MOBA_BLOCK], vg)
        return out

    out = lax.map(one_chunk, jnp.arange(seq // Q_BLOCK))
    return out.transpose(1, 0, 3, 2, 4).reshape(bsz, seq, ch).astype(q.dtype)


def retention_mix(q, k, v, g, cos, sin, gn_g, gn_b):
    bsz, seq, ch = q.shape
    H, D, L = N_GROUP_HEADS, HEAD_DIM, RET_CHUNK
    n_chunks = seq // L
    to_heads = lambda t: t.reshape(bsz, seq, H, D).transpose(0, 2, 1, 3)
    qh = apply_rope(to_heads(q), cos, sin).astype(jnp.float32)
    kh = apply_rope(to_heads(k), cos, sin).astype(jnp.float32) * D ** -0.5
    vh = to_heads(v).astype(jnp.float32)
    qc = qh.reshape(bsz, H, n_chunks, L, D)
    kc = kh.reshape(bsz, H, n_chunks, L, D)
    vc = vh.reshape(bsz, H, n_chunks, L, D)
    log_gamma = jnp.log1p(-jnp.power(2.0, -5.0 - jnp.arange(H, dtype=jnp.float32)))
    idx = jnp.arange(L, dtype=jnp.float32)
    diff = idx[:, None] - idx[None, :]
    decay_mask = jnp.where(diff >= 0, jnp.exp(log_gamma[:, None, None] * jnp.maximum(diff, 0.0)), 0.0)
    intra = jnp.einsum('bhcid,bhcjd->bhcij', qc, kc) * decay_mask[None, :, None]
    out = jnp.einsum('bhcij,bhcje->bhcie', intra, vc)
    k_w = jnp.exp(log_gamma[:, None] * (L - 1 - idx))
    kv = jnp.einsum('bhcjd,bhcje->bhcde', kc * k_w[None, :, None, :, None], vc)
    chunk_decay = jnp.exp(log_gamma * L)[None, :, None, None]

    def step(state, kv_c):
        return state * chunk_decay + kv_c, state

    _, states = lax.scan(step, jnp.zeros((bsz, H, D, D), jnp.float32), jnp.moveaxis(kv, 2, 0))
    states = jnp.moveaxis(states, 0, 2)
    q_w = jnp.exp(log_gamma[:, None] * (idx + 1.0))
    out = out + jnp.einsum('bhcid,bhcde->bhcie', qc * q_w[None, :, None, :, None], states)
    out = out.reshape(bsz, H, seq, D).transpose(0, 2, 1, 3)
    out = head_norm(out, gn_g, gn_b, GN_EPS).reshape(bsz, seq, ch)
    return (jax.nn.silu(g) * out).astype(q.dtype)


def conv_ffn(x, w_up, conv_w, conv_b, w_down):
    u = causal_depthwise_conv(x @ w_up, conv_w, conv_b)
    gate, val = jnp.split(u, 2, axis=-1)
    return (jax.nn.gelu(gate) * val) @ w_down


def setup_inputs(seed: int = 0) -> dict:
    key = jax.random.key(seed)
    ks = jax.random.split(key, 32)
    L, G, H, N = DEPTH, GROUP_WIDTH, N_GROUP_HEADS, HEAD_DIM

    def nrm(k, shape, scale):
        return jax.random.normal(k, shape, jnp.float32) * scale

    def gain(k, shape):
        return 1.0 + nrm(k, shape, 0.02)

    lru_u = jax.random.uniform(ks[17], (L, G), jnp.float32, 0.9, 0.999)
    lru_a = lru_u ** (1.0 / LRU_C)
    return {
        'x': nrm(ks[0], (BATCH, SEQ, D_MODEL), 1.0),
        'w_in': nrm(ks[1], (L, D_MODEL, IN_COLS), D_MODEL ** -0.5),
        'tshift_mu': jax.random.uniform(ks[2], (L, RWKV_COLS), jnp.float32),
        'rwkv_w0': jax.random.uniform(ks[3], (L, G), jnp.float32, -6.0, 0.0),
        'rwkv_w_up': nrm(ks[4], (L, RWKV_DECAY_RANK, G), 0.1 * RWKV_DECAY_RANK ** -0.5),
        'rwkv_a0': nrm(ks[5], (L, G), 0.1),
        'rwkv_a_up': nrm(ks[6], (L, RWKV_AAA_RANK, G), RWKV_AAA_RANK ** -0.5),
        'rwkv_g_up': nrm(ks[7], (L, RWKV_GATE_RANK, G), RWKV_GATE_RANK ** -0.5),
        'rwkv_k_k': 0.85 + nrm(ks[8], (L, G), 0.02),
        'rwkv_k_a': 1.0 + nrm(ks[9], (L, G), 0.02),
        'rwkv_r_k': nrm(ks[10], (L, H, N), 0.1),
        'rwkv_gn_g': gain(ks[11], (L, G)),
        'rwkv_gn_b': nrm(ks[12], (L, G), 0.02),
        'lru_conv_w': nrm(ks[13], (L, LRU_CONV_WIDTH, G), LRU_CONV_WIDTH ** -0.5),
        'lru_conv_b': nrm(ks[14], (L, G), 0.02),
        'lru_w_r': nrm(ks[15], (L, H, N, N), N ** -0.5),
        'lru_b_r': nrm(ks[16], (L, G), 0.02),
        'lru_w_i': nrm(ks[18], (L, H, N, N), N ** -0.5),
        'lru_b_i': nrm(ks[19], (L, G), 0.02),
        'lru_lambda': jnp.log(lru_a) - jnp.log1p(-lru_a),
        'ret_gn_g': gain(ks[20], (L, G)),
        'ret_gn_b': nrm(ks[21], (L, G), 0.02),
        'w_out': nrm(ks[22], (L, D_MIX, D_MODEL), DEEPNORM_BETA * D_MIX ** -0.5),
        'ln1_g': gain(ks[23], (L, D_MODEL)),
        'ln1_b': nrm(ks[24], (L, D_MODEL), 0.02),
        'ffn_w_up': nrm(ks[25], (L, D_MODEL, 2 * D_FF), D_MODEL ** -0.5),
        'ffn_conv_w': nrm(ks[26], (L, FFN_CONV_WIDTH, 2 * D_FF), FFN_CONV_WIDTH ** -0.5),
        'ffn_conv_b': nrm(ks[27], (L, 2 * D_FF), 0.02),
        'ffn_w_down': nrm(ks[28], (L, D_FF, D_MODEL), DEEPNORM_BETA * D_FF ** -0.5),
        'ln2_g': gain(ks[29], (L, D_MODEL)),
        'ln2_b': nrm(ks[30], (L, D_MODEL), 0.02),
    }


def reference(x, w_in, tshift_mu, rwkv_w0, rwkv_w_up, rwkv_a0, rwkv_a_up, rwkv_g_up,
              rwkv_k_k, rwkv_k_a, rwkv_r_k, rwkv_gn_g, rwkv_gn_b, lru_conv_w, lru_conv_b,
              lru_w_r, lru_b_r, lru_w_i, lru_b_i, lru_lambda, ret_gn_g, ret_gn_b, w_out,
              ln1_g, ln1_b, ffn_w_up, ffn_conv_w, ffn_conv_b, ffn_w_down, ln2_g, ln2_b):
    cos, sin = rope_tables(x.shape[1], HEAD_DIM)
    for l in range(DEPTH):
        p = x @ w_in[l]
        (p_rwkv, lru_x, lru_gate, att_q, att_k, att_v,
         ret_q, ret_k, ret_v, ret_g) = jnp.split(p, IN_SPLITS, axis=-1)
        y_a = rwkv7_mix(p_rwkv, tshift_mu[l], rwkv_w0[l], rwkv_w_up[l], rwkv_a0[l], rwkv_a_up[l],
                        rwkv_g_up[l], rwkv_k_k[l], rwkv_k_a[l], rwkv_r_k[l], rwkv_gn_g[l], rwkv_gn_b[l])
        y_b = rglru_mix(lru_x, lru_gate, lru_conv_w[l], lru_conv_b[l], lru_w_r[l], lru_b_r[l],
                        lru_w_i[l], lru_b_i[l], lru_lambda[l])
        y_c = moba_attention(att_q, att_k, att_v, cos, sin)
        y_d = retention_mix(ret_q, ret_k, ret_v, ret_g, cos, sin, ret_gn_g[l], ret_gn_b[l])
        mixed = jnp.concatenate([y_a, y_b, y_c, y_d], axis=-1) @ w_out[l]
        x = layer_norm(DEEPNORM_ALPHA * x + mixed, ln1_g[l], ln1_b[l])
        ffn = conv_ffn(x, ffn_w_up[l], ffn_conv_w[l], ffn_conv_b[l], ffn_w_down[l])
        x = layer_norm(DEEPNORM_ALPHA * x + ffn, ln2_g[l], ln2_b[l])
    return x
```

```python
import functools
import math

import jax
import jax.numpy as jnp
from jax import lax
from jax.experimental import pallas as pl
from jax.experimental.pallas import tpu as pltpu

F32 = jnp.float32
BF16 = jnp.bfloat16
HIGHEST = lax.Precision.HIGHEST

D_MODEL = 1024
HEAD_DIM = 64
GROUP = 256
HEADS = GROUP // HEAD_DIM
RWKV_COLS = 3 * GROUP + 32 + 32 + 64
RWKV_LR = RWKV_COLS - 3 * GROUP
RWKV_GN_EPS = 64e-5
LRU_C = 8.0
MOBA_BLOCK = 256
MOBA_TOPK = 3
ROPE_THETA = 10000.0
NEG_INF = -1e30
GN_EPS = 1e-5
D_FF = 2816
LN_EPS = 1e-5
RET_CHUNK = 256

VMEM_LIMIT_BYTES = 56 * 1024 * 1024
SUBLANES = 8


def _cparams(*semantics):
    return pltpu.CompilerParams(dimension_semantics=semantics, vmem_limit_bytes=VMEM_LIMIT_BYTES)


def _hdot(a, b):
    return jnp.dot(a, b, precision=HIGHEST, preferred_element_type=F32)


def _bdot(a, b):
    return jnp.dot(a.astype(BF16), b.astype(BF16), preferred_element_type=F32)


def _bdot_t(a, b):
    return lax.dot_general(a.astype(BF16), b.astype(BF16), (((1,), (1,)), ((), ())),
                           preferred_element_type=F32)


def _softplus(z):
    return jnp.maximum(z, 0.0) + jnp.log1p(jnp.exp(-jnp.abs(z)))


def _gelu_tanh(x):
    return 0.5 * x * (1.0 + jnp.tanh(math.sqrt(2.0 / math.pi) * (x + 0.044715 * (x * x * x))))


def _shift_rows(x, halo, s):
    rolled = pltpu.roll(x, s, axis=0)
    row8 = lax.broadcasted_iota(jnp.int32, halo.shape, 0)
    top = jnp.where(row8 < s, pltpu.roll(halo, s, axis=0), rolled[0:SUBLANES])
    return jnp.concatenate([top, rolled[SUBLANES:]], axis=0)


def _layer_norm(y, g, b):
    mu = jnp.mean(y, axis=-1, keepdims=True)
    d = y - mu
    var = jnp.mean(d * d, axis=-1, keepdims=True)
    return d * lax.rsqrt(var + LN_EPS) * g + b


def _halo_map(tm):
    return lambda i: (jnp.maximum(i * (tm // SUBLANES) - 1, 0), 0)


IN_WIDTHS = (RWKV_COLS, 2 * GROUP, 3 * GROUP, 4 * GROUP)
IN_COLS = sum(IN_WIDTHS)


def _in_proj_kernel(x_ref, w_ref, *outs):
    xb = x_ref[...].astype(BF16)
    col = 0
    for o in outs:
        n = o.shape[1]
        o[...] = jnp.dot(xb, w_ref[:, col:col + n], preferred_element_type=F32)
        col += n


def _in_proj(x2d, w_bf16, tm=512):
    m = x2d.shape[0]
    return pl.pallas_call(
        _in_proj_kernel,
        grid=(m // tm,),
        in_specs=[pl.BlockSpec((tm, D_MODEL), lambda i: (i, 0)),
                  pl.BlockSpec((D_MODEL, IN_COLS), lambda i: (0, 0))],
        out_specs=[pl.BlockSpec((tm, w), lambda i: (i, 0)) for w in IN_WIDTHS],
        out_shape=[jax.ShapeDtypeStruct((m, w), F32) for w in IN_WIDTHS],
        compiler_params=_cparams("arbitrary"),
        name="in_proj",
    )(x2d, w_bf16)


def _rwkv_prep_kernel(p_ref, halo_ref, mu_ref, pv_ref, lrw_ref, ones_ref,
                      r_o, k_o, v_o, kk_o, b_o, w_o, g_o, bv_o, *, tiles_per_seq):
    i = pl.program_id(0)
    pa = p_ref[...]
    prev = jnp.where(i % tiles_per_seq == 0, 0.0, halo_ref[SUBLANES - 1:SUBLANES, :])
    row = lax.broadcasted_iota(jnp.int32, pa.shape, 0)
    shifted = jnp.where(row == 0, prev, pltpu.roll(pa, 1, axis=0))
    pa = pa + (shifted - pa) * mu_ref[...]
    r = pa[:, 0:GROUP]
    k = pa[:, GROUP:2 * GROUP]
    v = pa[:, 2 * GROUP:3 * GROUP]
    lr = pa[:, 3 * GROUP:RWKV_COLS]
    w0, a0, k_k, k_a, r_k = (pv_ref[n:n + 1, :] for n in range(5))
    tw = _hdot(jnp.tanh(lr), lrw_ref[0])
    ta = _hdot(lr, lrw_ref[1])
    g = _hdot(jax.nn.sigmoid(lr), lrw_ref[2])
    w_log = -_softplus(-(w0 + tw)) - 0.5
    decay = jnp.exp(-jnp.exp(w_log))
    a = jax.nn.sigmoid(a0 + ta)
    kk = k * k_k
    k2 = k * (1.0 + (a - 1.0) * k_a)
    ss = _hdot(kk * kk, ones_ref[...])
    kkn = kk / jnp.maximum(jnp.sqrt(ss), 1e-12)
    bonus = _hdot(r * k2 * r_k, ones_ref[...])
    r_o[...] = r
    k_o[...] = k2
    v_o[...] = v
    kk_o[...] = kkn
    b_o[...] = kkn * a
    w_o[...] = decay
    g_o[...] = g
    bv_o[...] = bonus * v


def _rwkv_prep(p_rwkv, mu, pv, lrw, ones_blk, seq, tm=256):
    m = p_rwkv.shape[0]
    full = lambda a: pl.BlockSpec(a.shape, lambda i: (0,) * a.ndim)
    return pl.pallas_call(
        functools.partial(_rwkv_prep_kernel, tiles_per_seq=seq // tm),
        grid=(m // tm,),
        in_specs=[pl.BlockSpec((tm, RWKV_COLS), lambda i: (i, 0)),
                  pl.BlockSpec((SUBLANES, RWKV_COLS), _halo_map(tm)),
                  full(mu), full(pv), full(lrw), full(ones_blk)],
        out_specs=[pl.BlockSpec((tm, GROUP), lambda i: (i, 0))] * 8,
        out_shape=[jax.ShapeDtypeStruct((m, GROUP), F32)] * 8,
        compiler_params=_cparams("arbitrary"),
        name="rwkv_prep",
    )(p_rwkv, p_rwkv, mu, pv, lrw, ones_blk)


def _rwkv_scan_kernel(r_ref, k_ref, v_ref, kk_ref, b_ref, w_ref, y_ref, s_ref, *, tb, nb):
    n_jlo = s_ref.shape[0]
    lanes = s_ref.shape[2]

    @pl.when(pl.program_id(0) == 0)
    def _():
        s_ref[...] = jnp.zeros_like(s_ref)

    lane = lax.broadcasted_iota(jnp.int32, (HEAD_DIM, lanes), 1)
    upper = (lane // nb) % 2 == 1

    def both_halves(x):
        return x + jnp.where(upper, pltpu.roll(x, nb, axis=1), pltpu.roll(x, lanes - nb, axis=1))

    def step(t, carry):
        acc = jnp.zeros((HEAD_DIM, lanes), F32)
        for j in range(n_jlo):
            acc = acc + s_ref[j] * kk_ref[t, pl.ds(j, 1), :]
        sa = both_halves(acc)
        vt = v_ref[t]
        yacc = jnp.zeros((HEAD_DIM, lanes), F32)
        for j in range(n_jlo):
            sj = (s_ref[j] * w_ref[t, pl.ds(j, 1), :] - sa * b_ref[t, pl.ds(j, 1), :]
                  + vt * k_ref[t, pl.ds(j, 1), :])
            s_ref[j] = sj
            yacc = yacc + sj * r_ref[t, pl.ds(j, 1), :]
        y_ref[t] = both_halves(yacc)
        return carry

    lax.fori_loop(0, tb, step, 0)


def _rwkv_scan(r, k, v, kk, b, w, nb, tb=64):
    seq, n_jlo, lanes = r.shape
    kspec = pl.BlockSpec((tb, n_jlo, lanes), lambda i: (i, 0, 0))
    vspec = pl.BlockSpec((tb, HEAD_DIM, lanes), lambda i: (i, 0, 0))
    return pl.pallas_call(
        functools.partial(_rwkv_scan_kernel, tb=tb, nb=nb),
        grid=(seq // tb,),
        in_specs=[kspec, kspec, vspec, kspec, kspec, kspec],
        out_specs=vspec,
        out_shape=jax.ShapeDtypeStruct((seq, HEAD_DIM, lanes), F32),
        scratch_shapes=[pltpu.VMEM((n_jlo, HEAD_DIM, lanes), F32)],
        compiler_params=_cparams("arbitrary"),
        name="rwkv_scan",
    )(r, k, v, kk, b, w)


def _to_scan_key(x2d, nb, seq):
    x = x2d.reshape(nb, seq, HEADS, 2, HEAD_DIM // 2)
    return x.transpose(1, 4, 2, 3, 0).reshape(seq, HEAD_DIM // 2, HEADS * 2 * nb)


def _to_scan_val(x2d, nb, seq):
    x = x2d.reshape(nb, seq, HEADS, HEAD_DIM).transpose(1, 3, 2, 0)
    x = jnp.broadcast_to(x[:, :, :, None, :], (seq, HEAD_DIM, HEADS, 2, nb))
    return x.reshape(seq, HEAD_DIM, HEADS * 2 * nb)


def _from_scan(y, nb, seq):
    y = y.reshape(seq, HEAD_DIM, HEADS, 2, nb)[:, :, :, 0, :]
    return y.transpose(3, 0, 2, 1).reshape(nb * seq, GROUP)


def _rwkv_post_kernel(y_ref, g_ref, bv_ref, gn_ref, ones_ref, o_ref):
    y = y_ref[...]
    inv_n = 1.0 / HEAD_DIM
    mu = _hdot(y, ones_ref[...]) * inv_n
    d = y - mu
    var = _hdot(d * d, ones_ref[...]) * inv_n
    yn = d * lax.rsqrt(var + RWKV_GN_EPS) * gn_ref[0:1, :] + gn_ref[1:2, :]
    o_ref[...] = (yn + bv_ref[...]) * g_ref[...]


def _rwkv_post(y, g, bv, gn, ones_blk, tm=512):
    m = y.shape[0]
    row = pl.BlockSpec((tm, GROUP), lambda i: (i, 0))
    full = lambda a: pl.BlockSpec(a.shape, lambda i: (0,) * a.ndim)
    return pl.pallas_call(
        _rwkv_post_kernel,
        grid=(m // tm,),
        in_specs=[row, row, row, full(gn), full(ones_blk)],
        out_specs=row,
        out_shape=jax.ShapeDtypeStruct((m, GROUP), F32),
        compiler_params=_cparams("arbitrary"),
        name="rwkv_post",
    )(y, g, bv, gn, ones_blk)


def _lru_kernel(p_ref, halo_ref, cw_ref, pv_ref, wr_ref, wi_ref, o_ref, hc_ref, a_s, u_s, h_s, *, tt):
    j = pl.program_id(1)
    x = p_ref[:, 0:GROUP]
    gb = p_ref[:, GROUP:2 * GROUP]
    halo = jnp.where(j == 0, 0.0, halo_ref[:, 0:GROUP])
    conv_b, b_r, b_i, lam = (pv_ref[n:n + 1, :] for n in range(4))
    xc = x * cw_ref[3:4, :] + conv_b
    for s in (1, 2, 3):
        xc = xc + _shift_rows(x, halo, s) * cw_ref[3 - s:4 - s, :]
    gate_r = jax.nn.sigmoid(_bdot(xc, wr_ref[...]) + b_r)
    gate_i = jax.nn.sigmoid(_bdot(xc, wi_ref[...]) + b_i)
    log_a = -LRU_C * gate_r * _softplus(-lam)
    z = 2.0 * log_a
    expm1_z = jnp.tanh(0.5 * z) * (jnp.exp(z) + 1.0)
    a_s[...] = jnp.exp(log_a)
    u_s[...] = jnp.sqrt(-expm1_z) * (gate_i * xc)

    @pl.when(j == 0)
    def _():
        hc_ref[...] = jnp.zeros_like(hc_ref)

    def step(t, h):
        h = a_s[pl.ds(t, 1), :] * h + u_s[pl.ds(t, 1), :]
        h_s[pl.ds(t, 1), :] = h
        return h

    hc_ref[...] = lax.fori_loop(0, tt, step, hc_ref[...])
    o_ref[...] = _gelu_tanh(gb) * h_s[...]


def _lru(p_lru, cw, pv, wr, wi, nb, seq, tt=256):
    m = p_lru.shape[0]
    nt = seq // tt
    full = lambda a: pl.BlockSpec(a.shape, lambda b, j: (0,) * a.ndim)
    halo = _halo_map(tt)
    return pl.pallas_call(
        functools.partial(_lru_kernel, tt=tt),
        grid=(nb, nt),
        in_specs=[pl.BlockSpec((tt, 2 * GROUP), lambda b, j: (b * nt + j, 0)),
                  pl.BlockSpec((SUBLANES, 2 * GROUP), lambda b, j: halo(b * nt + j)),
                  full(cw), full(pv), full(wr), full(wi)],
        out_specs=pl.BlockSpec((tt, GROUP), lambda b, j: (b * nt + j, 0)),
        out_shape=jax.ShapeDtypeStruct((m, GROUP), F32),
        scratch_shapes=[pltpu.VMEM((1, GROUP), F32)] + [pltpu.VMEM((tt, GROUP), F32)] * 3,
        compiler_params=_cparams("arbitrary", "arbitrary"),
        name="lru",
    )(p_lru, p_lru, cw, pv, wr, wi)


def _rope_rows(x, cos, sin_signed):
    lane = lax.broadcasted_iota(jnp.int32, x.shape, 1)
    first_half = lane % HEAD_DIM < HEAD_DIM // 2
    width = x.shape[1]
    partner = jnp.where(first_half, pltpu.roll(x, width - HEAD_DIM // 2, axis=1),
                        pltpu.roll(x, HEAD_DIM // 2, axis=1))
    return x * cos + partner * sin_signed


def _rope_kernel(p_ref, cos_ref, sin_ref, q_o, k_o, km_o):
    cos = cos_ref[...]
    sin = sin_ref[...]
    q_o[...] = _rope_rows(p_ref[:, 0:GROUP], cos, sin)
    k = _rope_rows(p_ref[:, GROUP:2 * GROUP], cos, sin)
    k_o[...] = k
    km_o[0] = jnp.mean(k, axis=0, keepdims=True)


def _rope(p, cos, sin, seq):
    m = p.shape[0]
    tm = MOBA_BLOCK
    nt = seq // tm
    row = pl.BlockSpec((tm, GROUP), lambda i: (i, 0))
    tab = pl.BlockSpec((tm, GROUP), lambda i: (i % nt, 0))
    return pl.pallas_call(
        _rope_kernel,
        grid=(m // tm,),
        in_specs=[pl.BlockSpec((tm, 2 * GROUP), lambda i: (i, 0)), tab, tab],
        out_specs=[row, row, pl.BlockSpec((1, 1, GROUP), lambda i: (i, 0, 0))],
        out_shape=[jax.ShapeDtypeStruct((m, GROUP), F32), jax.ShapeDtypeStruct((m, GROUP), F32),
                   jax.ShapeDtypeStruct((m // tm, 1, GROUP), F32)],
        compiler_params=_cparams("arbitrary"),
        name="rope",
    )(p, cos, sin)


def _rope_tables(seq):
    inv = ROPE_THETA ** (-jnp.arange(0, HEAD_DIM, 2, dtype=F32) / HEAD_DIM)
    ang = jnp.arange(seq, dtype=F32)[:, None] * inv[None, :]
    cos, sin = jnp.cos(ang), jnp.sin(ang)
    cos_full = jnp.tile(jnp.concatenate([cos, cos], axis=1), (1, HEADS))
    sin_signed = jnp.tile(jnp.concatenate([-sin, sin], axis=1), (1, HEADS))
    return cos_full, sin_signed


def _moba_kernel(q_ref, k_ref, v_ref, km_ref, o_ref):
    blk = pl.program_id(1)
    tq = q_ref.shape[0]
    n_blocks = km_ref.shape[0]
    scale = HEAD_DIM ** -0.5
    row = lax.broadcasted_iota(jnp.int32, (tq, MOBA_BLOCK), 0)
    col = lax.broadcasted_iota(jnp.int32, (tq, MOBA_BLOCK), 1)
    causal = col <= row
    blk_id = lax.broadcasted_iota(jnp.int32, (tq, n_blocks), 1)
    valid = blk_id < blk
    own = pl.multiple_of(blk * MOBA_BLOCK, MOBA_BLOCK)

    for h in range(HEADS):
        hs = slice(h * HEAD_DIM, (h + 1) * HEAD_DIM)
        q = q_ref[:, hs]
        gate = lax.dot_general(q, km_ref[:, hs], (((1,), (1,)), ((), ())),
                               precision=HIGHEST, preferred_element_type=F32)
        sel = jnp.zeros((tq, n_blocks), F32)
        for n in range(n_blocks):
            gn = gate[:, n:n + 1]
            beats = valid & ((gate > gn) | ((gate == gn) & (blk_id < n)))
            cnt = jnp.sum(beats.astype(F32), axis=1, keepdims=True)
            sel = jnp.where((blk_id == n) & (cnt < MOBA_TOPK) & valid, 1.0, sel)

        qb = q.astype(BF16)
        s = _bdot_t(qb, k_ref[pl.ds(own, MOBA_BLOCK), hs]) * scale
        s = jnp.where(causal, s, NEG_INF)
        m0 = jnp.max(s, axis=1, keepdims=True)
        p = jnp.exp(s - m0)
        l0 = jnp.sum(p, axis=1, keepdims=True)
        acc0 = _bdot(p, v_ref[pl.ds(own, MOBA_BLOCK), hs])

        def body(n, carry):
            m_run, l_run, acc = carry
            start = pl.multiple_of(n * MOBA_BLOCK, MOBA_BLOCK)
            picked = jnp.sum(jnp.where(blk_id == n, sel, 0.0), axis=1, keepdims=True) > 0.5
            s = _bdot_t(qb, k_ref[pl.ds(start, MOBA_BLOCK), hs]) * scale
            s = jnp.where(picked, s, NEG_INF)
            m_new = jnp.maximum(m_run, jnp.max(s, axis=1, keepdims=True))
            alpha = jnp.exp(m_run - m_new)
            p = jnp.exp(s - m_new)
            l_new = alpha * l_run + jnp.sum(p, axis=1, keepdims=True)
            acc = alpha * acc + _bdot(p, v_ref[pl.ds(start, MOBA_BLOCK), hs])
            return m_new, l_new, acc

        _, l_fin, acc = lax.fori_loop(0, blk, body, (m0, l0, acc0))
        o_ref[:, hs] = acc / l_fin


def _moba(q_r, k_r, p_att, kmean, nb, seq):
    m = q_r.shape[0]
    nblk = seq // MOBA_BLOCK
    return pl.pallas_call(
        _moba_kernel,
        grid=(nb, nblk),
        in_specs=[pl.BlockSpec((MOBA_BLOCK, GROUP), lambda b, j: (b * nblk + j, 0)),
                  pl.BlockSpec((seq, GROUP), lambda b, j: (b, 0)),
                  pl.BlockSpec((seq, GROUP), lambda b, j: (b, 2)),
                  pl.BlockSpec((None, nblk, GROUP), lambda b, j: (b, 0, 0))],
        out_specs=pl.BlockSpec((MOBA_BLOCK, GROUP), lambda b, j: (b * nblk + j, 0)),
        out_shape=jax.ShapeDtypeStruct((m, GROUP), F32),
        compiler_params=_cparams("arbitrary", "arbitrary"),
        name="moba",
    )(q_r, k_r, p_att, kmean.reshape(nb, nblk, GROUP))


def _ret_kernel(q_ref, k_ref, vg_ref, gn_ref, o_ref, st_ref):
    lc = q_ref.shape[0]

    @pl.when(pl.program_id(1) == 0)
    def _():
        st_ref[...] = jnp.zeros_like(st_ref)

    ri = lax.broadcasted_iota(jnp.int32, (lc, lc), 0)
    ci = lax.broadcasted_iota(jnp.int32, (lc, lc), 1)
    diff = (ri - ci).astype(F32)
    pos = lax.broadcasted_iota(jnp.int32, (lc, 1), 0).astype(F32)
    for h in range(HEADS):
        hs = slice(h * HEAD_DIM, (h + 1) * HEAD_DIM)
        log_gamma = math.log1p(-(2.0 ** (-5.0 - h)))
        q = q_ref[:, hs]
        k = k_ref[:, hs] * HEAD_DIM ** -0.5
        v = vg_ref[:, hs]
        g = vg_ref[:, GROUP + h * HEAD_DIM:GROUP + (h + 1) * HEAD_DIM]
        mask = jnp.where(diff >= 0, jnp.exp(log_gamma * jnp.maximum(diff, 0.0)), 0.0)
        intra = _bdot_t(q, k) * mask
        state = st_ref[h]
        out = _bdot(intra, v) + _bdot(q * jnp.exp(log_gamma * (pos + 1.0)), state)
        k_w = jnp.exp(log_gamma * (lc - 1.0 - pos))
        kv = lax.dot_general((k * k_w).astype(BF16), v.astype(BF16), (((0,), (0,)), ((), ())),
                             preferred_element_type=F32)
        st_ref[h] = state * math.exp(log_gamma * lc) + kv
        mu = jnp.mean(out, axis=-1, keepdims=True)
        d = out - mu
        var = jnp.mean(d * d, axis=-1, keepdims=True)
        yn = d * lax.rsqrt(var + GN_EPS) * gn_ref[0:1, hs] + gn_ref[1:2, hs]
        o_ref[:, hs] = g * jax.nn.sigmoid(g) * yn


def _retention(q_r, k_r, p_ret, gn, nb, seq):
    m = q_r.shape[0]
    lc = RET_CHUNK
    nt = seq // lc
    row = pl.BlockSpec((lc, GROUP), lambda b, j: (b * nt + j, 0))
    return pl.pallas_call(
        _ret_kernel,
        grid=(nb, nt),
        in_specs=[row, row,
                  pl.BlockSpec((lc, 2 * GROUP), lambda b, j: (b * nt + j, 1)),
                  pl.BlockSpec(gn.shape, lambda b, j: (0, 0))],
        out_specs=row,
        out_shape=jax.ShapeDtypeStruct((m, GROUP), F32),
        scratch_shapes=[pltpu.VMEM((HEADS, HEAD_DIM, HEAD_DIM), F32)],
        compiler_params=_cparams("arbitrary", "arbitrary"),
        name="retention",
    )(q_r, k_r, p_ret, gn)


def _out_proj_kernel(ya_ref, yb_ref, yc_ref, yd_ref, x_ref, w_ref, ln_ref, o_ref, *, alpha):
    acc = alpha * x_ref[...]
    for n, y_ref in enumerate((ya_ref, yb_ref, yc_ref, yd_ref)):
        acc = acc + _bdot(y_ref[...], w_ref[n * GROUP:(n + 1) * GROUP, :])
    o_ref[...] = _layer_norm(acc, ln_ref[0:1, :], ln_ref[1:2, :])


def _out_proj(ys, x2d, w_bf16, ln, alpha, tm=512):
    m = x2d.shape[0]
    yspec = pl.BlockSpec((tm, GROUP), lambda i: (i, 0))
    xspec = pl.BlockSpec((tm, D_MODEL), lambda i: (i, 0))
    full = lambda a: pl.BlockSpec(a.shape, lambda i: (0,) * a.ndim)
    return pl.pallas_call(
        functools.partial(_out_proj_kernel, alpha=alpha),
        grid=(m // tm,),
        in_specs=[yspec] * 4 + [xspec, full(w_bf16), full(ln)],
        out_specs=xspec,
        out_shape=jax.ShapeDtypeStruct((m, D_MODEL), F32),
        compiler_params=_cparams("arbitrary"),
        name="out_proj_ln",
    )(*ys, x2d, w_bf16, ln)


FFN_TN = 256


def _ffn_up_kernel(x_ref, halo_ref, w_ref, cw_ref, cb_ref, o_ref, *, tiles_per_seq):
    i = pl.program_id(0)
    xb = x_ref[...].astype(BF16)
    hb = halo_ref[...].astype(BF16)
    first = i % tiles_per_seq == 0

    def conv_cols(c0):
        w = w_ref[:, c0:c0 + FFN_TN]
        h = jnp.dot(xb, w, preferred_element_type=F32)
        hh = jnp.where(first, 0.0, jnp.dot(hb, w, preferred_element_type=F32))
        u = h * cw_ref[2:3, c0:c0 + FFN_TN] + cb_ref[:, c0:c0 + FFN_TN]
        for s in (1, 2):
            u = u + _shift_rows(h, hh, s) * cw_ref[2 - s:3 - s, c0:c0 + FFN_TN]
        return u

    for c in range(D_FF // FFN_TN):
        gate = conv_cols(c * FFN_TN)
        val = conv_cols(D_FF + c * FFN_TN)
        o_ref[:, c * FFN_TN:(c + 1) * FFN_TN] = (_gelu_tanh(gate) * val).astype(BF16)


def _ffn_up(x2d, w_bf16, cw, cb, seq, tm=512):
    m = x2d.shape[0]
    full = lambda a: pl.BlockSpec(a.shape, lambda i: (0,) * a.ndim)
    return pl.pallas_call(
        functools.partial(_ffn_up_kernel, tiles_per_seq=seq // tm),
        grid=(m // tm,),
        in_specs=[pl.BlockSpec((tm, D_MODEL), lambda i: (i, 0)),
                  pl.BlockSpec((SUBLANES, D_MODEL), _halo_map(tm)),
                  full(w_bf16), full(cw), full(cb)],
        out_specs=pl.BlockSpec((tm, D_FF), lambda i: (i, 0)),
        out_shape=jax.ShapeDtypeStruct((m, D_FF), BF16),
        compiler_params=_cparams("arbitrary"),
        name="ffn_up",
    )(x2d, x2d, w_bf16, cw, cb)


def _ffn_down_kernel(a_ref, x_ref, w_ref, ln_ref, o_ref, *, alpha):
    acc = alpha * x_ref[...] + jnp.dot(a_ref[...], w_ref[...], preferred_element_type=F32)
    o_ref[...] = _layer_norm(acc, ln_ref[0:1, :], ln_ref[1:2, :])


def _ffn_down(act, x2d, w_bf16, ln, alpha, tm=512):
    m = x2d.shape[0]
    xspec = pl.BlockSpec((tm, D_MODEL), lambda i: (i, 0))
    full = lambda a: pl.BlockSpec(a.shape, lambda i: (0,) * a.ndim)
    return pl.pallas_call(
        functools.partial(_ffn_down_kernel, alpha=alpha),
        grid=(m // tm,),
        in_specs=[pl.BlockSpec((tm, D_FF), lambda i: (i, 0)), xspec, full(w_bf16), full(ln)],
        out_specs=xspec,
        out_shape=jax.ShapeDtypeStruct((m, D_MODEL), F32),
        compiler_params=_cparams("arbitrary"),
        name="ffn_down_ln",
    )(act, x2d, w_bf16, ln)


def _block_diag(w):
    h, n, _ = w.shape
    eye = jnp.eye(h, dtype=w.dtype)
    return (eye[:, None, :, None] * w[:, :, None, :]).reshape(h * n, h * n)


def _pad_rows(w, offset, total):
    return jnp.zeros((total, w.shape[1]), w.dtype).at[offset:offset + w.shape[0]].set(w)


def _rows8(*vecs):
    rows = [v.reshape(1, -1) for v in vecs]
    rows += [jnp.zeros_like(rows[0])] * (SUBLANES - len(rows))
    return jnp.concatenate(rows, axis=0)


def kernel(x, w_in, tshift_mu, rwkv_w0, rwkv_w_up, rwkv_a0, rwkv_a_up, rwkv_g_up, rwkv_k_k, rwkv_k_a, rwkv_r_k, rwkv_gn_g, rwkv_gn_b, lru_conv_w, lru_conv_b, lru_w_r, lru_b_r, lru_w_i, lru_b_i, lru_lambda, ret_gn_g, ret_gn_b, w_out, ln1_g, ln1_b, ffn_w_up, ffn_conv_w, ffn_conv_b, ffn_w_down, ln2_g, ln2_b):
    nb, seq, _ = x.shape
    depth = w_in.shape[0]
    alpha = (2.0 * depth) ** 0.25
    cos, sin = _rope_tables(seq)
    ones_blk = _block_diag(jnp.ones((HEADS, HEAD_DIM, HEAD_DIM), F32))
    x2d = x.reshape(nb * seq, D_MODEL)
    for l in range(depth):
        p_rwkv, p_lru, p_att, p_ret = _in_proj(x2d, w_in[l].astype(BF16))

        lrw = jnp.stack([_pad_rows(rwkv_w_up[l], 0, RWKV_LR), _pad_rows(rwkv_a_up[l], 32, RWKV_LR),
                         _pad_rows(rwkv_g_up[l], 64, RWKV_LR)])
        pv = _rows8(rwkv_w0[l], rwkv_a0[l], rwkv_k_k[l], rwkv_k_a[l], rwkv_r_k[l])
        r, k2, v, kkn, bvec, decay, g, bv = _rwkv_prep(p_rwkv, tshift_mu[l].reshape(1, -1), pv, lrw,
                                                       ones_blk, seq)
        tk = lambda a: _to_scan_key(a, nb, seq)
        y = _rwkv_scan(tk(r), tk(k2), _to_scan_val(v, nb, seq), tk(kkn), tk(bvec), tk(decay), nb)
        y_a = _rwkv_post(_from_scan(y, nb, seq), g, bv, _rows8(rwkv_gn_g[l], rwkv_gn_b[l]), ones_blk)

        y_b = _lru(p_lru, lru_conv_w[l], _rows8(lru_conv_b[l], lru_b_r[l], lru_b_i[l], lru_lambda[l]),
                   _block_diag(lru_w_r[l]).astype(BF16), _block_diag(lru_w_i[l]).astype(BF16), nb, seq)

        q_r, k_r, kmean = _rope(p_att, cos, sin, seq)
        y_c = _moba(q_r, k_r, p_att, kmean, nb, seq)

        q_r, k_r, _ = _rope(p_ret, cos, sin, seq)
        y_d = _retention(q_r, k_r, p_ret, _rows8(ret_gn_g[l], ret_gn_b[l]), nb, seq)

        x2d = _out_proj((y_a, y_b, y_c, y_d), x2d, w_out[l].astype(BF16), _rows8(ln1_g[l], ln1_b[l]), alpha)
        act = _ffn_up(x2d, ffn_w_up[l].astype(BF16), ffn_conv_w[l], ffn_conv_b[l].reshape(1, -1), seq)
        x2d = _ffn_down(act, x2d, ffn_w_down[l].astype(BF16), _rows8(ln2_g[l], ln2_b[l]), alpha)
    return x2d.reshape(nb, seq, D_MODEL)
```

```python
import functools
import math

import jax
import jax.numpy as jnp
from jax import lax
from jax.experimental import pallas as pl
from jax.experimental.pallas import tpu as pltpu

F32 = jnp.float32
BF16 = jnp.bfloat16
HIGHEST = lax.Precision.HIGHEST

D_MODEL = 1024
HEAD_DIM = 64
GROUP = 256
HEADS = GROUP // HEAD_DIM
RWKV_COLS = 3 * GROUP + 32 + 32 + 64
RWKV_LR = RWKV_COLS - 3 * GROUP
RWKV_GN_EPS = 64e-5
LRU_C = 8.0
MOBA_BLOCK = 256
MOBA_TOPK = 3
ROPE_THETA = 10000.0
NEG_INF = -1e30
GN_EPS = 1e-5
D_FF = 2816
LN_EPS = 1e-5
RET_CHUNK = 256

VMEM_LIMIT_BYTES = 56 * 1024 * 1024
SUBLANES = 8


def _cparams(*semantics):
    return pltpu.CompilerParams(dimension_semantics=semantics, vmem_limit_bytes=VMEM_LIMIT_BYTES)


def _hdot(a, b):
    return jnp.dot(a, b, precision=HIGHEST, preferred_element_type=F32)


def _bdot(a, b):
    return jnp.dot(a.astype(BF16), b.astype(BF16), preferred_element_type=F32)


def _bdot_t(a, b):
    return lax.dot_general(a.astype(BF16), b.astype(BF16), (((1,), (1,)), ((), ())),
                           preferred_element_type=F32)


def _softplus(z):
    return jnp.maximum(z, 0.0) + jnp.log1p(jnp.exp(-jnp.abs(z)))


def _gelu_tanh(x):
    return 0.5 * x * (1.0 + jnp.tanh(math.sqrt(2.0 / math.pi) * (x + 0.044715 * (x * x * x))))


def _shift_rows(x, halo, s):
    rolled = pltpu.roll(x, s, axis=0)
    row8 = lax.broadcasted_iota(jnp.int32, halo.shape, 0)
    top = jnp.where(row8 < s, pltpu.roll(halo, s, axis=0), rolled[0:SUBLANES])
    return jnp.concatenate([top, rolled[SUBLANES:]], axis=0)


def _layer_norm(y, g, b):
    mu = jnp.mean(y, axis=-1, keepdims=True)
    d = y - mu
    var = jnp.mean(d * d, axis=-1, keepdims=True)
    return d * lax.rsqrt(var + LN_EPS) * g + b


def _halo_map(tm):
    return lambda i: (jnp.maximum(i * (tm // SUBLANES) - 1, 0), 0)


IN_WIDTHS = (RWKV_COLS, 2 * GROUP, 3 * GROUP, 4 * GROUP)
IN_COLS = sum(IN_WIDTHS)


def _in_proj_kernel(x_ref, w_ref, *outs):
    xb = x_ref[...].astype(BF16)
    col = 0
    for o in outs:
        n = o.shape[1]
        o[...] = jnp.dot(xb, w_ref[:, col:col + n], preferred_element_type=F32)
        col += n


def _in_proj(x2d, w_bf16, tm=512):
    m = x2d.shape[0]
    return pl.pallas_call(
        _in_proj_kernel,
        grid=(m // tm,),
        in_specs=[pl.BlockSpec((tm, D_MODEL), lambda i: (i, 0)),
                  pl.BlockSpec((D_MODEL, IN_COLS), lambda i: (0, 0))],
        out_specs=[pl.BlockSpec((tm, w), lambda i: (i, 0)) for w in IN_WIDTHS],
        out_shape=[jax.ShapeDtypeStruct((m, w), F32) for w in IN_WIDTHS],
        compiler_params=_cparams("arbitrary"),
        name="in_proj",
    )(x2d, w_bf16)


def _rwkv_prep_kernel(p_ref, halo_ref, mu_ref, pv_ref, lrw_ref, ones_ref,
                      r_o, k_o, v_o, kk_o, b_o, w_o, g_o, bv_o, *, tiles_per_seq):
    i = pl.program_id(0)
    pa = p_ref[...]
    prev = jnp.where(i % tiles_per_seq == 0, 0.0, halo_ref[SUBLANES - 1:SUBLANES, :])
    row = lax.broadcasted_iota(jnp.int32, pa.shape, 0)
    shifted = jnp.where(row == 0, prev, pltpu.roll(pa, 1, axis=0))
    pa = pa + (shifted - pa) * mu_ref[...]
    r = pa[:, 0:GROUP]
    k = pa[:, GROUP:2 * GROUP]
    v = pa[:, 2 * GROUP:3 * GROUP]
    lr = pa[:, 3 * GROUP:RWKV_COLS]
    w0, a0, k_k, k_a, r_k = (pv_ref[n:n + 1, :] for n in range(5))
    tw = _hdot(jnp.tanh(lr), lrw_ref[0])
    ta = _hdot(lr, lrw_ref[1])
    g = _hdot(jax.nn.sigmoid(lr), lrw_ref[2])
    w_log = -_softplus(-(w0 + tw)) - 0.5
    decay = jnp.exp(-jnp.exp(w_log))
    a = jax.nn.sigmoid(a0 + ta)
    kk = k * k_k
    k2 = k * (1.0 + (a - 1.0) * k_a)
    ss = _hdot(kk * kk, ones_ref[...])
    kkn = kk / jnp.maximum(jnp.sqrt(ss), 1e-12)
    bonus = _hdot(r * k2 * r_k, ones_ref[...])
    r_o[...] = r
    k_o[...] = k2
    v_o[...] = v
    kk_o[...] = kkn
    b_o[...] = kkn * a
    w_o[...] = decay
    g_o[...] = g
    bv_o[...] = bonus * v


def _rwkv_prep(p_rwkv, mu, pv, lrw, ones_blk, seq, tm=256):
    m = p_rwkv.shape[0]
    full = lambda a: pl.BlockSpec(a.shape, lambda i: (0,) * a.ndim)
    return pl.pallas_call(
        functools.partial(_rwkv_prep_kernel, tiles_per_seq=seq // tm),
        grid=(m // tm,),
        in_specs=[pl.BlockSpec((tm, RWKV_COLS), lambda i: (i, 0)),
                  pl.BlockSpec((SUBLANES, RWKV_COLS), _halo_map(tm)),
                  full(mu), full(pv), full(lrw), full(ones_blk)],
        out_specs=[pl.BlockSpec((tm, GROUP), lambda i: (i, 0))] * 8,
        out_shape=[jax.ShapeDtypeStruct((m, GROUP), F32)] * 8,
        compiler_params=_cparams("arbitrary"),
        name="rwkv_prep",
    )(p_rwkv, p_rwkv, mu, pv, lrw, ones_blk)


def _rwkv_scan_kernel(r_ref, k_ref, v_ref, kk_ref, b_ref, w_ref, y_ref, s_ref, *, tb, nb):
    n_jlo = s_ref.shape[0]
    lanes = s_ref.shape[2]

    @pl.when(pl.program_id(0) == 0)
    def _():
        s_ref[...] = jnp.zeros_like(s_ref)

    lane = lax.broadcasted_iota(jnp.int32, (HEAD_DIM, lanes), 1)
    upper = (lane // nb) % 2 == 1

    def both_halves(x):
        return x + jnp.where(upper, pltpu.roll(x, nb, axis=1), pltpu.roll(x, lanes - nb, axis=1))

    acc = jnp.zeros((HEAD_DIM, lanes), F32)
    for j in range(n_jlo):
        acc = acc + s_ref[j] * kk_ref[0, pl.ds(j, 1), :]

    def step(t, sa):
        t_next = jnp.minimum(t + 1, tb - 1)
        vt = v_ref[t]
        yacc = jnp.zeros((HEAD_DIM, lanes), F32)
        nacc = jnp.zeros((HEAD_DIM, lanes), F32)
        for j in range(n_jlo):
            sj = (s_ref[j] * w_ref[t, pl.ds(j, 1), :] - sa * b_ref[t, pl.ds(j, 1), :]
                  + vt * k_ref[t, pl.ds(j, 1), :])
            s_ref[j] = sj
            yacc = yacc + sj * r_ref[t, pl.ds(j, 1), :]
            nacc = nacc + sj * kk_ref[t_next, pl.ds(j, 1), :]
        y_ref[t] = both_halves(yacc)
        return both_halves(nacc)

    lax.fori_loop(0, tb, step, both_halves(acc))


def _rwkv_scan(r, k, v, kk, b, w, nb, tb=64):
    seq, n_jlo, lanes = r.shape
    kspec = pl.BlockSpec((tb, n_jlo, lanes), lambda i: (i, 0, 0))
    vspec = pl.BlockSpec((tb, HEAD_DIM, lanes), lambda i: (i, 0, 0))
    return pl.pallas_call(
        functools.partial(_rwkv_scan_kernel, tb=tb, nb=nb),
        grid=(seq // tb,),
        in_specs=[kspec, kspec, vspec, kspec, kspec, kspec],
        out_specs=vspec,
        out_shape=jax.ShapeDtypeStruct((seq, HEAD_DIM, lanes), F32),
        scratch_shapes=[pltpu.VMEM((n_jlo, HEAD_DIM, lanes), F32)],
        compiler_params=_cparams("arbitrary"),
        name="rwkv_scan",
    )(r, k, v, kk, b, w)


def _to_scan_key(x2d, nb, seq):
    x = x2d.reshape(nb, seq, HEADS, 2, HEAD_DIM // 2)
    return x.transpose(1, 4, 2, 3, 0).reshape(seq, HEAD_DIM // 2, HEADS * 2 * nb)


def _to_scan_val(x2d, nb, seq):
    x = x2d.reshape(nb, seq, HEADS, HEAD_DIM).transpose(1, 3, 2, 0)
    x = jnp.broadcast_to(x[:, :, :, None, :], (seq, HEAD_DIM, HEADS, 2, nb))
    return x.reshape(seq, HEAD_DIM, HEADS * 2 * nb)


def _from_scan(y, nb, seq):
    y = y.reshape(seq, HEAD_DIM, HEADS, 2, nb)[:, :, :, 0, :]
    return y.transpose(3, 0, 2, 1).reshape(nb * seq, GROUP)


def _rwkv_post_kernel(y_ref, g_ref, bv_ref, gn_ref, ones_ref, o_ref):
    y = y_ref[...]
    inv_n = 1.0 / HEAD_DIM
    mu = _hdot(y, ones_ref[...]) * inv_n
    d = y - mu
    var = _hdot(d * d, ones_ref[...]) * inv_n
    yn = d * lax.rsqrt(var + RWKV_GN_EPS) * gn_ref[0:1, :] + gn_ref[1:2, :]
    o_ref[...] = (yn + bv_ref[...]) * g_ref[...]


def _rwkv_post(y, g, bv, gn, ones_blk, tm=512):
    m = y.shape[0]
    row = pl.BlockSpec((tm, GROUP), lambda i: (i, 0))
    full = lambda a: pl.BlockSpec(a.shape, lambda i: (0,) * a.ndim)
    return pl.pallas_call(
        _rwkv_post_kernel,
        grid=(m // tm,),
        in_specs=[row, row, row, full(gn), full(ones_blk)],
        out_specs=row,
        out_shape=jax.ShapeDtypeStruct((m, GROUP), F32),
        compiler_params=_cparams("arbitrary"),
        name="rwkv_post",
    )(y, g, bv, gn, ones_blk)


def _lru_kernel(p_ref, halo_ref, cw_ref, pv_ref, wr_ref, wi_ref, o_ref, hc_ref, *, tt):
    j = pl.program_id(1)
    x = p_ref[:, 0:GROUP]
    gb = p_ref[:, GROUP:2 * GROUP]
    halo = jnp.where(j == 0, 0.0, halo_ref[:, 0:GROUP])
    conv_b, b_r, b_i, lam = (pv_ref[n:n + 1, :] for n in range(4))
    xc = x * cw_ref[3:4, :] + conv_b
    for s in (1, 2, 3):
        xc = xc + _shift_rows(x, halo, s) * cw_ref[3 - s:4 - s, :]
    gate_r = jax.nn.sigmoid(_bdot(xc, wr_ref[...]) + b_r)
    gate_i = jax.nn.sigmoid(_bdot(xc, wi_ref[...]) + b_i)
    log_a = -LRU_C * gate_r * _softplus(-lam)
    z = 2.0 * log_a
    expm1_z = jnp.tanh(0.5 * z) * (jnp.exp(z) + 1.0)
    a = jnp.exp(log_a)
    u = jnp.sqrt(-expm1_z) * (gate_i * xc)

    @pl.when(j == 0)
    def _():
        hc_ref[...] = jnp.zeros_like(hc_ref)

    row = lax.broadcasted_iota(jnp.int32, a.shape, 0)
    d = 1
    while d < tt:
        keep = row >= d
        u = a * jnp.where(keep, pltpu.roll(u, d, axis=0), 0.0) + u
        a = a * jnp.where(keep, pltpu.roll(a, d, axis=0), 1.0)
        d *= 2
    h = a * hc_ref[...] + u
    hc_ref[...] = h[tt - 1:tt, :]
    o_ref[...] = _gelu_tanh(gb) * h


def _lru(p_lru, cw, pv, wr, wi, nb, seq, tt=256):
    m = p_lru.shape[0]
    nt = seq // tt
    full = lambda a: pl.BlockSpec(a.shape, lambda b, j: (0,) * a.ndim)
    halo = _halo_map(tt)
    return pl.pallas_call(
        functools.partial(_lru_kernel, tt=tt),
        grid=(nb, nt),
        in_specs=[pl.BlockSpec((tt, 2 * GROUP), lambda b, j: (b * nt + j, 0)),
                  pl.BlockSpec((SUBLANES, 2 * GROUP), lambda b, j: halo(b * nt + j)),
                  full(cw), full(pv), full(wr), full(wi)],
        out_specs=pl.BlockSpec((tt, GROUP), lambda b, j: (b * nt + j, 0)),
        out_shape=jax.ShapeDtypeStruct((m, GROUP), F32),
        scratch_shapes=[pltpu.VMEM((1, GROUP), F32)],
        compiler_params=_cparams("arbitrary", "arbitrary"),
        name="lru",
    )(p_lru, p_lru, cw, pv, wr, wi)


def _rope_rows(x, cos, sin_signed):
    lane = lax.broadcasted_iota(jnp.int32, x.shape, 1)
    first_half = lane % HEAD_DIM < HEAD_DIM // 2
    width = x.shape[1]
    partner = jnp.where(first_half, pltpu.roll(x, width - HEAD_DIM // 2, axis=1),
                        pltpu.roll(x, HEAD_DIM // 2, axis=1))
    return x * cos + partner * sin_signed


def _rope_kernel(p_ref, cos_ref, sin_ref, q_o, k_o):
    cos = cos_ref[...]
    sin = sin_ref[...]
    q_o[...] = _rope_rows(p_ref[:, 0:GROUP], cos, sin)
    k_o[...] = _rope_rows(p_ref[:, GROUP:2 * GROUP], cos, sin)


def _rope(p, cos, sin, seq, tm=256):
    m = p.shape[0]
    nt = seq // tm
    row = pl.BlockSpec((tm, GROUP), lambda i: (i, 0))
    tab = pl.BlockSpec((tm, GROUP), lambda i: (i % nt, 0))
    return pl.pallas_call(
        _rope_kernel,
        grid=(m // tm,),
        in_specs=[pl.BlockSpec((tm, 2 * GROUP), lambda i: (i, 0)), tab, tab],
        out_specs=[row, row],
        out_shape=[jax.ShapeDtypeStruct((m, GROUP), F32)] * 2,
        compiler_params=_cparams("arbitrary"),
        name="rope",
    )(p, cos, sin)


def _rope_moba_kernel(p_ref, cos_ref, sin_ref, qt_o, k_o, vt_o, km_o):
    cos = cos_ref[...]
    sin = sin_ref[...]
    q = _rope_rows(p_ref[:, 0:GROUP], cos, sin) * HEAD_DIM ** -0.5
    qt_o[...] = q.T
    k = _rope_rows(p_ref[:, GROUP:2 * GROUP], cos, sin)
    for h in range(HEADS):
        k_o[h] = k[:, h * HEAD_DIM:(h + 1) * HEAD_DIM].astype(BF16)
    km_o[0] = jnp.mean(k, axis=0, keepdims=True)
    vt_o[...] = p_ref[:, 2 * GROUP:3 * GROUP].T.astype(BF16)


def _rope_moba(p, cos, sin, seq):
    m = p.shape[0]
    tm = MOBA_BLOCK
    nt = seq // tm
    tab = pl.BlockSpec((tm, GROUP), lambda i: (i % nt, 0))
    col = pl.BlockSpec((GROUP, tm), lambda i: (0, i))
    return pl.pallas_call(
        _rope_moba_kernel,
        grid=(m // tm,),
        in_specs=[pl.BlockSpec((tm, 3 * GROUP), lambda i: (i, 0)), tab, tab],
        out_specs=[col, pl.BlockSpec((HEADS, tm, HEAD_DIM), lambda i: (0, i, 0)), col,
                   pl.BlockSpec((1, 1, GROUP), lambda i: (i, 0, 0))],
        out_shape=[jax.ShapeDtypeStruct((GROUP, m), F32), jax.ShapeDtypeStruct((HEADS, m, HEAD_DIM), BF16),
                   jax.ShapeDtypeStruct((GROUP, m), BF16), jax.ShapeDtypeStruct((m // tm, 1, GROUP), F32)],
        compiler_params=_cparams("arbitrary"),
        name="rope_moba",
    )(p, cos, sin)


def _rope_tables(seq):
    inv = ROPE_THETA ** (-jnp.arange(0, HEAD_DIM, 2, dtype=F32) / HEAD_DIM)
    ang = jnp.arange(seq, dtype=F32)[:, None] * inv[None, :]
    cos, sin = jnp.cos(ang), jnp.sin(ang)
    cos_full = jnp.tile(jnp.concatenate([cos, cos], axis=1), (1, HEADS))
    sin_signed = jnp.tile(jnp.concatenate([-sin, sin], axis=1), (1, HEADS))
    return cos_full, sin_signed


def _moba_kernel(qt_ref, k_ref, vt_ref, km_ref, o_ref, m_s, l_s, acc_s):
    blk = pl.program_id(1)
    tq = qt_ref.shape[1]
    n_blocks = km_ref.shape[0]
    key_i = lax.broadcasted_iota(jnp.int32, (MOBA_BLOCK, tq), 0)
    qry_i = lax.broadcasted_iota(jnp.int32, (MOBA_BLOCK, tq), 1)
    causal = key_i <= qry_i
    blk_id = lax.broadcasted_iota(jnp.int32, (n_blocks, tq), 0)
    valid = blk_id < blk
    own = pl.multiple_of(blk * MOBA_BLOCK, MOBA_BLOCK)
    heads = [slice(h * HEAD_DIM, (h + 1) * HEAD_DIM) for h in range(HEADS)]

    sels = []
    for h, hs in enumerate(heads):
        qt = qt_ref[hs, :]
        gate = _hdot(km_ref[:, hs], qt)
        sel = jnp.zeros((n_blocks, tq), F32)
        for n in range(n_blocks):
            gn = gate[n:n + 1, :]
            beats = valid & ((gate > gn) | ((gate == gn) & (blk_id < n)))
            cnt = jnp.sum(beats.astype(F32), axis=0, keepdims=True)
            sel = jnp.where((blk_id == n) & (cnt < MOBA_TOPK) & valid, 1.0, sel)
        sels.append(sel)

        s = jnp.dot(k_ref[h, pl.ds(own, MOBA_BLOCK), :], qt.astype(BF16), preferred_element_type=F32)
        s = jnp.where(causal, s, NEG_INF)
        m0 = jnp.max(s, axis=0, keepdims=True)
        p = jnp.exp(s - m0)
        m_s[h] = m0
        l_s[h] = jnp.sum(p, axis=0, keepdims=True)
        acc_s[hs, :] = jnp.dot(vt_ref[hs, pl.ds(own, MOBA_BLOCK)], p.astype(BF16),
                               preferred_element_type=F32)

    def body(n, carry):
        start = pl.multiple_of(n * MOBA_BLOCK, MOBA_BLOCK)
        for h, hs in enumerate(heads):
            picked = jnp.sum(jnp.where(blk_id == n, sels[h], 0.0), axis=0, keepdims=True) > 0.5
            s = jnp.dot(k_ref[h, pl.ds(start, MOBA_BLOCK), :], qt_ref[hs, :].astype(BF16),
                        preferred_element_type=F32)
            s = jnp.where(picked, s, NEG_INF)
            m_run = m_s[h]
            m_new = jnp.maximum(m_run, jnp.max(s, axis=0, keepdims=True))
            alpha = jnp.exp(m_run - m_new)
            p = jnp.exp(s - m_new)
            m_s[h] = m_new
            l_s[h] = alpha * l_s[h] + jnp.sum(p, axis=0, keepdims=True)
            acc_s[hs, :] = alpha * acc_s[hs, :] + jnp.dot(vt_ref[hs, pl.ds(start, MOBA_BLOCK)], p.astype(BF16),
                                                          preferred_element_type=F32)
        return carry

    lax.fori_loop(0, blk, body, 0)
    out_t = jnp.concatenate([acc_s[hs, :] / l_s[h] for h, hs in enumerate(heads)], axis=0)
    o_ref[...] = out_t.T


def _moba(qt, kh, vt, kmean, nb, seq):
    m = qt.shape[1]
    nblk = seq // MOBA_BLOCK
    return pl.pallas_call(
        _moba_kernel,
        grid=(nb, nblk),
        in_specs=[pl.BlockSpec((GROUP, MOBA_BLOCK), lambda b, j: (0, b * nblk + j)),
                  pl.BlockSpec((HEADS, seq, HEAD_DIM), lambda b, j: (0, b, 0)),
                  pl.BlockSpec((GROUP, seq), lambda b, j: (0, b)),
                  pl.BlockSpec((None, nblk, GROUP), lambda b, j: (b, 0, 0))],
        out_specs=pl.BlockSpec((MOBA_BLOCK, GROUP), lambda b, j: (b * nblk + j, 0)),
        out_shape=jax.ShapeDtypeStruct((m, GROUP), F32),
        scratch_shapes=[pltpu.VMEM((HEADS, 1, MOBA_BLOCK), F32), pltpu.VMEM((HEADS, 1, MOBA_BLOCK), F32),
                        pltpu.VMEM((GROUP, MOBA_BLOCK), F32)],
        compiler_params=_cparams("arbitrary", "arbitrary"),
        name="moba",
    )(qt, kh, vt, kmean.reshape(nb, nblk, GROUP))


def _ret_kernel(q_ref, k_ref, vg_ref, gn_ref, o_ref, st_ref):
    lc = q_ref.shape[0]

    @pl.when(pl.program_id(1) == 0)
    def _():
        st_ref[...] = jnp.zeros_like(st_ref)

    ri = lax.broadcasted_iota(jnp.int32, (lc, lc), 0)
    ci = lax.broadcasted_iota(jnp.int32, (lc, lc), 1)
    diff = (ri - ci).astype(F32)
    pos = lax.broadcasted_iota(jnp.int32, (lc, 1), 0).astype(F32)
    for h in range(HEADS):
        hs = slice(h * HEAD_DIM, (h + 1) * HEAD_DIM)
        log_gamma = math.log1p(-(2.0 ** (-5.0 - h)))
        q = q_ref[:, hs]
        k = k_ref[:, hs] * HEAD_DIM ** -0.5
        v = vg_ref[:, hs]
        g = vg_ref[:, GROUP + h * HEAD_DIM:GROUP + (h + 1) * HEAD_DIM]
        mask = jnp.where(diff >= 0, jnp.exp(log_gamma * jnp.maximum(diff, 0.0)), 0.0)
        intra = _bdot_t(q, k) * mask
        state = st_ref[h]
        out = _bdot(intra, v) + _bdot(q * jnp.exp(log_gamma * (pos + 1.0)), state)
        k_w = jnp.exp(log_gamma * (lc - 1.0 - pos))
        kv = lax.dot_general((k * k_w).astype(BF16), v.astype(BF16), (((0,), (0,)), ((), ())),
                             preferred_element_type=F32)
        st_ref[h] = state * math.exp(log_gamma * lc) + kv
        mu = jnp.mean(out, axis=-1, keepdims=True)
        d = out - mu
        var = jnp.mean(d * d, axis=-1, keepdims=True)
        yn = d * lax.rsqrt(var + GN_EPS) * gn_ref[0:1, hs] + gn_ref[1:2, hs]
        o_ref[:, hs] = g * jax.nn.sigmoid(g) * yn


def _retention(q_r, k_r, p_ret, gn, nb, seq):
    m = q_r.shape[0]
    lc = RET_CHUNK
    nt = seq // lc
    row = pl.BlockSpec((lc, GROUP), lambda b, j: (b * nt + j, 0))
    return pl.pallas_call(
        _ret_kernel,
        grid=(nb, nt),
        in_specs=[row, row,
                  pl.BlockSpec((lc, 2 * GROUP), lambda b, j: (b * nt + j, 1)),
                  pl.BlockSpec(gn.shape, lambda b, j: (0, 0))],
        out_specs=row,
        out_shape=jax.ShapeDtypeStruct((m, GROUP), F32),
        scratch_shapes=[pltpu.VMEM((HEADS, HEAD_DIM, HEAD_DIM), F32)],
        compiler_params=_cparams("arbitrary", "arbitrary"),
        name="retention",
    )(q_r, k_r, p_ret, gn)


def _out_proj_kernel(ya_ref, yb_ref, yc_ref, yd_ref, x_ref, w_ref, ln_ref, o_ref, *, alpha):
    acc = alpha * x_ref[...]
    for n, y_ref in enumerate((ya_ref, yb_ref, yc_ref, yd_ref)):
        acc = acc + _bdot(y_ref[...], w_ref[n * GROUP:(n + 1) * GROUP, :])
    o_ref[...] = _layer_norm(acc, ln_ref[0:1, :], ln_ref[1:2, :])


def _out_proj(ys, x2d, w_bf16, ln, alpha, tm=512):
    m = x2d.shape[0]
    yspec = pl.BlockSpec((tm, GROUP), lambda i: (i, 0))
    xspec = pl.BlockSpec((tm, D_MODEL), lambda i: (i, 0))
    full = lambda a: pl.BlockSpec(a.shape, lambda i: (0,) * a.ndim)
    return pl.pallas_call(
        functools.partial(_out_proj_kernel, alpha=alpha),
        grid=(m // tm,),
        in_specs=[yspec] * 4 + [xspec, full(w_bf16), full(ln)],
        out_specs=xspec,
        out_shape=jax.ShapeDtypeStruct((m, D_MODEL), F32),
        compiler_params=_cparams("arbitrary"),
        name="out_proj_ln",
    )(*ys, x2d, w_bf16, ln)


FFN_TN = 256


def _ffn_up_kernel(x_ref, halo_ref, w_ref, cw_ref, cb_ref, o_ref, *, tiles_per_seq):
    i = pl.program_id(0)
    xb = x_ref[...].astype(BF16)
    hb = halo_ref[...].astype(BF16)
    first = i % tiles_per_seq == 0

    def conv_cols(c0):
        w = w_ref[:, c0:c0 + FFN_TN]
        h = jnp.dot(xb, w, preferred_element_type=F32)
        hh = jnp.where(first, 0.0, jnp.dot(hb, w, preferred_element_type=F32))
        u = h * cw_ref[2:3, c0:c0 + FFN_TN] + cb_ref[:, c0:c0 + FFN_TN]
        for s in (1, 2):
            u = u + _shift_rows(h, hh, s) * cw_ref[2 - s:3 - s, c0:c0 + FFN_TN]
        return u

    for c in range(D_FF // FFN_TN):
        gate = conv_cols(c * FFN_TN)
        val = conv_cols(D_FF + c * FFN_TN)
        o_ref[:, c * FFN_TN:(c + 1) * FFN_TN] = (_gelu_tanh(gate) * val).astype(BF16)


def _ffn_up(x2d, w_bf16, cw, cb, seq, tm=512):
    m = x2d.shape[0]
    full = lambda a: pl.BlockSpec(a.shape, lambda i: (0,) * a.ndim)
    return pl.pallas_call(
        functools.partial(_ffn_up_kernel, tiles_per_seq=seq // tm),
        grid=(m // tm,),
        in_specs=[pl.BlockSpec((tm, D_MODEL), lambda i: (i, 0)),
                  pl.BlockSpec((SUBLANES, D_MODEL), _halo_map(tm)),
                  full(w_bf16), full(cw), full(cb)],
        out_specs=pl.BlockSpec((tm, D_FF), lambda i: (i, 0)),
        out_shape=jax.ShapeDtypeStruct((m, D_FF), BF16),
        compiler_params=_cparams("arbitrary"),
        name="ffn_up",
    )(x2d, x2d, w_bf16, cw, cb)


def _ffn_down_kernel(a_ref, x_ref, w_ref, ln_ref, o_ref, *, alpha):
    acc = alpha * x_ref[...] + jnp.dot(a_ref[...], w_ref[...], preferred_element_type=F32)
    o_ref[...] = _layer_norm(acc, ln_ref[0:1, :], ln_ref[1:2, :])


def _ffn_down(act, x2d, w_bf16, ln, alpha, tm=512):
    m = x2d.shape[0]
    xspec = pl.BlockSpec((tm, D_MODEL), lambda i: (i, 0))
    full = lambda a: pl.BlockSpec(a.shape, lambda i: (0,) * a.ndim)
    return pl.pallas_call(
        functools.partial(_ffn_down_kernel, alpha=alpha),
        grid=(m // tm,),
        in_specs=[pl.BlockSpec((tm, D_FF), lambda i: (i, 0)), xspec, full(w_bf16), full(ln)],
        out_specs=xspec,
        out_shape=jax.ShapeDtypeStruct((m, D_MODEL), F32),
        compiler_params=_cparams("arbitrary"),
        name="ffn_down_ln",
    )(act, x2d, w_bf16, ln)


def _block_diag(w):
    h, n, _ = w.shape
    eye = jnp.eye(h, dtype=w.dtype)
    return (eye[:, None, :, None] * w[:, :, None, :]).reshape(h * n, h * n)


def _pad_rows(w, offset, total):
    return jnp.zeros((total, w.shape[1]), w.dtype).at[offset:offset + w.shape[0]].set(w)


def _rows8(*vecs):
    rows = [v.reshape(1, -1) for v in vecs]
    rows += [jnp.zeros_like(rows[0])] * (SUBLANES - len(rows))
    return jnp.concatenate(rows, axis=0)


def kernel(x, w_in, tshift_mu, rwkv_w0, rwkv_w_up, rwkv_a0, rwkv_a_up, rwkv_g_up, rwkv_k_k, rwkv_k_a, rwkv_r_k, rwkv_gn_g, rwkv_gn_b, lru_conv_w, lru_conv_b, lru_w_r, lru_b_r, lru_w_i, lru_b_i, lru_lambda, ret_gn_g, ret_gn_b, w_out, ln1_g, ln1_b, ffn_w_up, ffn_conv_w, ffn_conv_b, ffn_w_down, ln2_g, ln2_b):
    nb, seq, _ = x.shape
    depth = w_in.shape[0]
    alpha = (2.0 * depth) ** 0.25
    cos, sin = _rope_tables(seq)
    ones_blk = _block_diag(jnp.ones((HEADS, HEAD_DIM, HEAD_DIM), F32))
    x2d = x.reshape(nb * seq, D_MODEL)
    for l in range(depth):
        p_rwkv, p_lru, p_att, p_ret = _in_proj(x2d, w_in[l].astype(BF16))

        lrw = jnp.stack([_pad_rows(rwkv_w_up[l], 0, RWKV_LR), _pad_rows(rwkv_a_up[l], 32, RWKV_LR),
                         _pad_rows(rwkv_g_up[l], 64, RWKV_LR)])
        pv = _rows8(rwkv_w0[l], rwkv_a0[l], rwkv_k_k[l], rwkv_k_a[l], rwkv_r_k[l])
        r, k2, v, kkn, bvec, decay, g, bv = _rwkv_prep(p_rwkv, tshift_mu[l].reshape(1, -1), pv, lrw,
                                                       ones_blk, seq)
        tk = lambda a: _to_scan_key(a, nb, seq)
        y = _rwkv_scan(tk(r), tk(k2), _to_scan_val(v, nb, seq), tk(kkn), tk(bvec), tk(decay), nb)
        y_a = _rwkv_post(_from_scan(y, nb, seq), g, bv, _rows8(rwkv_gn_g[l], rwkv_gn_b[l]), ones_blk)

        y_b = _lru(p_lru, lru_conv_w[l], _rows8(lru_conv_b[l], lru_b_r[l], lru_b_i[l], lru_lambda[l]),
                   _block_diag(lru_w_r[l]).astype(BF16), _block_diag(lru_w_i[l]).astype(BF16), nb, seq)

        y_c = _moba(*_rope_moba(p_att, cos, sin, seq), nb, seq)

        q_r, k_r = _rope(p_ret, cos, sin, seq)
        y_d = _retention(q_r, k_r, p_ret, _rows8(ret_gn_g[l], ret_gn_b[l]), nb, seq)

        x2d = _out_proj((y_a, y_b, y_c, y_d), x2d, w_out[l].astype(BF16), _rows8(ln1_g[l], ln1_b[l]), alpha)
        act = _ffn_up(x2d, ffn_w_up[l].astype(BF16), ffn_conv_w[l], ffn_conv_b[l].reshape(1, -1), seq)
        x2d = _ffn_down(act, x2d, ffn_w_down[l].astype(BF16), _rows8(ln2_g[l], ln2_b[l]), alpha)
    return x2d.reshape(nb, seq, D_MODEL)
```

```python
import functools
import math

import jax
import jax.numpy as jnp
from jax import lax
from jax.experimental import pallas as pl
from jax.experimental.pallas import tpu as pltpu

F32 = jnp.float32
BF16 = jnp.bfloat16
HIGHEST = lax.Precision.HIGHEST

D_MODEL = 1024
HEAD_DIM = 64
GROUP = 256
HEADS = GROUP // HEAD_DIM
RWKV_COLS = 3 * GROUP + 32 + 32 + 64
RWKV_LR = RWKV_COLS - 3 * GROUP
RWKV_GN_EPS = 64e-5
LRU_C = 8.0
MOBA_BLOCK = 256
MOBA_TOPK = 3
ROPE_THETA = 10000.0
NEG_INF = -1e30
GN_EPS = 1e-5
D_FF = 2816
LN_EPS = 1e-5
RET_CHUNK = 256

VMEM_LIMIT_BYTES = 56 * 1024 * 1024
SUBLANES = 8
LANES = 128


def _cparams(*semantics):
    return pltpu.CompilerParams(dimension_semantics=semantics, vmem_limit_bytes=VMEM_LIMIT_BYTES)


def _hdot(a, b):
    return jnp.dot(a, b, precision=HIGHEST, preferred_element_type=F32)


def _bdot(a, b):
    return jnp.dot(a.astype(BF16), b.astype(BF16), preferred_element_type=F32)


def _bdot_t(a, b):
    return lax.dot_general(a.astype(BF16), b.astype(BF16), (((1,), (1,)), ((), ())),
                           preferred_element_type=F32)


def _softplus(z):
    return jnp.maximum(z, 0.0) + jnp.log1p(jnp.exp(-jnp.abs(z)))


def _gelu_tanh(x):
    return 0.5 * x * (1.0 + jnp.tanh(math.sqrt(2.0 / math.pi) * (x + 0.044715 * (x * x * x))))


def _shift_rows(x, halo, s):
    rolled = pltpu.roll(x, s, axis=0)
    row8 = lax.broadcasted_iota(jnp.int32, halo.shape, 0)
    top = jnp.where(row8 < s, pltpu.roll(halo, s, axis=0), rolled[0:SUBLANES])
    return jnp.concatenate([top, rolled[SUBLANES:]], axis=0)


def _layer_norm(y, g, b):
    mu = jnp.mean(y, axis=-1, keepdims=True)
    d = y - mu
    var = jnp.mean(d * d, axis=-1, keepdims=True)
    return d * lax.rsqrt(var + LN_EPS) * g + b


def _halo_map(tm):
    return lambda i: (jnp.maximum(i * (tm // SUBLANES) - 1, 0), 0)


IN_WIDTHS = (RWKV_COLS, 2 * GROUP, 3 * GROUP, 4 * GROUP)
IN_COLS = sum(IN_WIDTHS)


def _in_proj_kernel(x_ref, w_ref, *outs):
    xb = x_ref[...].astype(BF16)
    col = 0
    for o in outs:
        n = o.shape[1]
        o[...] = jnp.dot(xb, w_ref[:, col:col + n], preferred_element_type=F32)
        col += n


def _in_proj(x2d, w_bf16, tm=512):
    m = x2d.shape[0]
    return pl.pallas_call(
        _in_proj_kernel,
        grid=(m // tm,),
        in_specs=[pl.BlockSpec((tm, D_MODEL), lambda i: (i, 0)),
                  pl.BlockSpec((D_MODEL, IN_COLS), lambda i: (0, 0))],
        out_specs=[pl.BlockSpec((tm, w), lambda i: (i, 0)) for w in IN_WIDTHS],
        out_shape=[jax.ShapeDtypeStruct((m, w), F32) for w in IN_WIDTHS],
        compiler_params=_cparams("arbitrary"),
        name="in_proj",
    )(x2d, w_bf16)


def _rows_to_lanes(x_s, out_ref, t0, nb, tb, gw, dup):
    steps = LANES // gw
    lane_grp = lax.broadcasted_iota(jnp.int32, (nb, LANES), 1) // gw
    z = [None] * (GROUP // gw)
    for tt in range(steps):
        xt = [x_s[c, pl.ds(t0 + tt, nb, stride=tb), :] for c in range(GROUP // LANES)]
        for g in range(len(z)):
            half = xt[g // steps]
            shift = (tt - g % steps) % steps * gw
            piece = pltpu.roll(half, shift, axis=1) if shift else half
            z[g] = piece if tt == 0 else jnp.where(lane_grp == tt, piece, z[g])
    tile = jnp.concatenate([zg for zg in z for _ in range(dup)], axis=0)
    out_ref[pl.ds(t0, steps)] = tile.T.reshape(steps, gw, tile.shape[0])


def _lanes_to_rows(y_ref, y_s, t0, nb, tb):
    steps = LANES // HEAD_DIM
    tile = y_ref[pl.ds(t0, steps)]
    tile = tile.reshape(steps * HEAD_DIM, tile.shape[2]).T
    lane_grp = lax.broadcasted_iota(jnp.int32, (nb, LANES), 1) // HEAD_DIM
    for tt in range(steps):
        for half in range(GROUP // LANES):
            cols = None
            for pos in range(steps):
                h = half * steps + pos
                piece = tile[h * 2 * nb:h * 2 * nb + nb, :]
                shift = (pos - tt) % steps * HEAD_DIM
                piece = pltpu.roll(piece, shift, axis=1) if shift else piece
                cols = piece if pos == 0 else jnp.where(lane_grp == pos, piece, cols)
            y_s[half, pl.ds(t0 + tt, nb, stride=tb), :] = cols


def _rwkv_prep_kernel(p_ref, halo_ref, mu_ref, pv_ref, lrw_ref, ones_ref,
                      r_o, k_o, v_o, kk_o, b_o, w_o, g_o, bv_o, *stage, nb, tb):
    rows = nb * tb
    pa = p_ref[...].reshape(rows, RWKV_COLS)
    prev = jnp.where(pl.program_id(0) == 0, 0.0, halo_ref[:, SUBLANES - 1:SUBLANES, :])
    prev = jnp.broadcast_to(prev, (nb, tb, RWKV_COLS)).reshape(rows, RWKV_COLS)
    t_idx = lax.broadcasted_iota(jnp.int32, pa.shape, 0) % tb
    shifted = jnp.where(t_idx == 0, prev, pltpu.roll(pa, 1, axis=0))
    pa = pa + (shifted - pa) * mu_ref[...]
    r = pa[:, 0:GROUP]
    k = pa[:, GROUP:2 * GROUP]
    v = pa[:, 2 * GROUP:3 * GROUP]
    lr = pa[:, 3 * GROUP:RWKV_COLS]
    w0, a0, k_k, k_a, r_k = (pv_ref[n:n + 1, :] for n in range(5))
    tw = _hdot(jnp.tanh(lr), lrw_ref[0])
    ta = _hdot(lr, lrw_ref[1])
    g = _hdot(jax.nn.sigmoid(lr), lrw_ref[2])
    w_log = -_softplus(-(w0 + tw)) - 0.5
    decay = jnp.exp(-jnp.exp(w_log))
    a = jax.nn.sigmoid(a0 + ta)
    kk = k * k_k
    k2 = k * (1.0 + (a - 1.0) * k_a)
    ss = _hdot(kk * kk, ones_ref[...])
    kkn = kk / jnp.maximum(jnp.sqrt(ss), 1e-12)
    bonus = _hdot(r * k2 * r_k, ones_ref[...])
    g_o[...] = g.reshape(nb, tb, GROUP)
    bv_o[...] = (bonus * v).reshape(nb, tb, GROUP)
    layouts = ((r, r_o, HEAD_DIM // 2, 1), (k2, k_o, HEAD_DIM // 2, 1), (v, v_o, HEAD_DIM, 2),
               (kkn, kk_o, HEAD_DIM // 2, 1), (kkn * a, b_o, HEAD_DIM // 2, 1), (decay, w_o, HEAD_DIM // 2, 1))
    for (x, _, _, _), x_s in zip(layouts, stage):
        for c in range(GROUP // LANES):
            x_s[c] = x[:, c * LANES:(c + 1) * LANES]
    span = LANES // (HEAD_DIM // 2)

    def move(n, carry):
        for (_, out, gw, dup), x_s in zip(layouts, stage):
            for t0 in range(0, span, LANES // gw):
                _rows_to_lanes(x_s, out, n * span + t0, nb, tb, gw, dup)
        return carry

    lax.fori_loop(0, tb // span, move, 0)


RWKV_TB = 32


def _rwkv_prep(p_rwkv, mu, pv, lrw, ones_blk, nb, seq, tb=RWKV_TB):
    lanes = 2 * HEADS * nb
    full = lambda a: pl.BlockSpec(a.shape, lambda i: (0,) * a.ndim)
    key_spec = pl.BlockSpec((tb, HEAD_DIM // 2, lanes), lambda i: (i, 0, 0))
    val_spec = pl.BlockSpec((tb, HEAD_DIM, lanes), lambda i: (i, 0, 0))
    nat_spec = pl.BlockSpec((nb, tb, GROUP), lambda i: (0, i, 0))
    key_shape = jax.ShapeDtypeStruct((seq, HEAD_DIM // 2, lanes), F32)
    val_shape = jax.ShapeDtypeStruct((seq, HEAD_DIM, lanes), F32)
    nat_shape = jax.ShapeDtypeStruct((nb, seq, GROUP), F32)
    return pl.pallas_call(
        functools.partial(_rwkv_prep_kernel, nb=nb, tb=tb),
        grid=(seq // tb,),
        in_specs=[pl.BlockSpec((nb, tb, RWKV_COLS), lambda i: (0, i, 0)),
                  pl.BlockSpec((nb, SUBLANES, RWKV_COLS),
                               lambda i: (0, jnp.maximum(i * (tb // SUBLANES) - 1, 0), 0)),
                  full(mu), full(pv), full(lrw), full(ones_blk)],
        out_specs=[key_spec, key_spec, val_spec, key_spec, key_spec, key_spec, nat_spec, nat_spec],
        out_shape=[key_shape, key_shape, val_shape, key_shape, key_shape, key_shape, nat_shape, nat_shape],
        scratch_shapes=[pltpu.VMEM((GROUP // LANES, nb * tb, LANES), F32)] * 6,
        compiler_params=_cparams("arbitrary"),
        name="rwkv_prep",
    )(p_rwkv, p_rwkv, mu, pv, lrw, ones_blk)


def _rwkv_scan_kernel(r_ref, k_ref, v_ref, kk_ref, b_ref, w_ref, y_ref, s_ref, *, tb, nb):
    n_jlo = s_ref.shape[0]
    lanes = s_ref.shape[2]

    @pl.when(pl.program_id(0) == 0)
    def _():
        s_ref[...] = jnp.zeros_like(s_ref)

    lane = lax.broadcasted_iota(jnp.int32, (HEAD_DIM, lanes), 1)
    upper = (lane // nb) % 2 == 1

    def both_halves(x):
        return x + jnp.where(upper, pltpu.roll(x, nb, axis=1), pltpu.roll(x, lanes - nb, axis=1))

    acc = jnp.zeros((HEAD_DIM, lanes), F32)
    for j in range(n_jlo):
        acc = acc + s_ref[j] * kk_ref[0, pl.ds(j, 1), :]

    def step(t, sa):
        t_next = jnp.minimum(t + 1, tb - 1)
        vt = v_ref[t]
        yacc = jnp.zeros((HEAD_DIM, lanes), F32)
        nacc = jnp.zeros((HEAD_DIM, lanes), F32)
        for j in range(n_jlo):
            sj = (s_ref[j] * w_ref[t, pl.ds(j, 1), :] - sa * b_ref[t, pl.ds(j, 1), :]
                  + vt * k_ref[t, pl.ds(j, 1), :])
            s_ref[j] = sj
            yacc = yacc + sj * r_ref[t, pl.ds(j, 1), :]
            nacc = nacc + sj * kk_ref[t_next, pl.ds(j, 1), :]
        y_ref[t] = both_halves(yacc)
        return both_halves(nacc)

    lax.fori_loop(0, tb, step, both_halves(acc))


def _rwkv_scan(r, k, v, kk, b, w, nb, tb=64):
    seq, n_jlo, lanes = r.shape
    kspec = pl.BlockSpec((tb, n_jlo, lanes), lambda i: (i, 0, 0))
    vspec = pl.BlockSpec((tb, HEAD_DIM, lanes), lambda i: (i, 0, 0))
    return pl.pallas_call(
        functools.partial(_rwkv_scan_kernel, tb=tb, nb=nb),
        grid=(seq // tb,),
        in_specs=[kspec, kspec, vspec, kspec, kspec, kspec],
        out_specs=vspec,
        out_shape=jax.ShapeDtypeStruct((seq, HEAD_DIM, lanes), F32),
        scratch_shapes=[pltpu.VMEM((n_jlo, HEAD_DIM, lanes), F32)],
        compiler_params=_cparams("arbitrary"),
        name="rwkv_scan",
    )(r, k, v, kk, b, w)


def _rwkv_post_kernel(y_ref, g_ref, bv_ref, gn_ref, ones_ref, o_ref, y_s, *, nb, tb):
    steps = LANES // HEAD_DIM
    unroll = 4

    def move(n, carry):
        for u in range(unroll):
            _lanes_to_rows(y_ref, y_s, (n * unroll + u) * steps, nb, tb)
        return carry

    lax.fori_loop(0, tb // (steps * unroll), move, 0)
    y = jnp.concatenate([y_s[c] for c in range(GROUP // LANES)], axis=1)
    inv_n = 1.0 / HEAD_DIM
    mu = _hdot(y, ones_ref[...]) * inv_n
    d = y - mu
    var = _hdot(d * d, ones_ref[...]) * inv_n
    yn = d * lax.rsqrt(var + RWKV_GN_EPS) * gn_ref[0:1, :] + gn_ref[1:2, :]
    out = (yn + bv_ref[...].reshape(nb * tb, GROUP)) * g_ref[...].reshape(nb * tb, GROUP)
    o_ref[...] = out.reshape(nb, tb, GROUP)


def _rwkv_post(y, g, bv, gn, ones_blk, nb, seq, tb=RWKV_TB):
    nat_spec = pl.BlockSpec((nb, tb, GROUP), lambda i: (0, i, 0))
    full = lambda a: pl.BlockSpec(a.shape, lambda i: (0,) * a.ndim)
    return pl.pallas_call(
        functools.partial(_rwkv_post_kernel, nb=nb, tb=tb),
        grid=(seq // tb,),
        in_specs=[pl.BlockSpec((tb, HEAD_DIM, y.shape[2]), lambda i: (i, 0, 0)), nat_spec, nat_spec,
                  full(gn), full(ones_blk)],
        out_specs=nat_spec,
        out_shape=jax.ShapeDtypeStruct((nb, seq, GROUP), F32),
        scratch_shapes=[pltpu.VMEM((GROUP // LANES, nb * tb, LANES), F32)],
        compiler_params=_cparams("arbitrary"),
        name="rwkv_post",
    )(y, g, bv, gn, ones_blk)


def _lru_kernel(p_ref, halo_ref, cw_ref, pv_ref, wr_ref, wi_ref, o_ref, hc_ref, *, tt):
    j = pl.program_id(1)
    x = p_ref[:, 0:GROUP]
    gb = p_ref[:, GROUP:2 * GROUP]
    halo = jnp.where(j == 0, 0.0, halo_ref[:, 0:GROUP])
    conv_b, b_r, b_i, lam = (pv_ref[n:n + 1, :] for n in range(4))
    xc = x * cw_ref[3:4, :] + conv_b
    for s in (1, 2, 3):
        xc = xc + _shift_rows(x, halo, s) * cw_ref[3 - s:4 - s, :]
    gate_r = jax.nn.sigmoid(_bdot(xc, wr_ref[...]) + b_r)
    gate_i = jax.nn.sigmoid(_bdot(xc, wi_ref[...]) + b_i)
    log_a = -LRU_C * gate_r * _softplus(-lam)
    z = 2.0 * log_a
    expm1_z = jnp.tanh(0.5 * z) * (jnp.exp(z) + 1.0)
    a = jnp.exp(log_a)
    u = jnp.sqrt(-expm1_z) * (gate_i * xc)

    @pl.when(j == 0)
    def _():
        hc_ref[...] = jnp.zeros_like(hc_ref)

    row = lax.broadcasted_iota(jnp.int32, a.shape, 0)
    d = 1
    while d < tt:
        keep = row >= d
        u = a * jnp.where(keep, pltpu.roll(u, d, axis=0), 0.0) + u
        a = a * jnp.where(keep, pltpu.roll(a, d, axis=0), 1.0)
        d *= 2
    h = a * hc_ref[...] + u
    hc_ref[...] = h[tt - 1:tt, :]
    o_ref[...] = _gelu_tanh(gb) * h


def _lru(p_lru, cw, pv, wr, wi, nb, seq, tt=256):
    m = p_lru.shape[0]
    nt = seq // tt
    full = lambda a: pl.BlockSpec(a.shape, lambda b, j: (0,) * a.ndim)
    halo = _halo_map(tt)
    return pl.pallas_call(
        functools.partial(_lru_kernel, tt=tt),
        grid=(nb, nt),
        in_specs=[pl.BlockSpec((tt, 2 * GROUP), lambda b, j: (b * nt + j, 0)),
                  pl.BlockSpec((SUBLANES, 2 * GROUP), lambda b, j: halo(b * nt + j)),
                  full(cw), full(pv), full(wr), full(wi)],
        out_specs=pl.BlockSpec((tt, GROUP), lambda b, j: (b * nt + j, 0)),
        out_shape=jax.ShapeDtypeStruct((m, GROUP), F32),
        scratch_shapes=[pltpu.VMEM((1, GROUP), F32)],
        compiler_params=_cparams("arbitrary", "arbitrary"),
        name="lru",
    )(p_lru, p_lru, cw, pv, wr, wi)


def _rope_rows(x, cos, sin_signed):
    lane = lax.broadcasted_iota(jnp.int32, x.shape, 1)
    first_half = lane % HEAD_DIM < HEAD_DIM // 2
    width = x.shape[1]
    partner = jnp.where(first_half, pltpu.roll(x, width - HEAD_DIM // 2, axis=1),
                        pltpu.roll(x, HEAD_DIM // 2, axis=1))
    return x * cos + partner * sin_signed


def _rope_moba_kernel(p_ref, cos_ref, sin_ref, qt_o, k_o, vt_o, km_o):
    cos = cos_ref[...]
    sin = sin_ref[...]
    q = _rope_rows(p_ref[:, 0:GROUP], cos, sin) * HEAD_DIM ** -0.5
    qt_o[...] = q.T
    k = _rope_rows(p_ref[:, GROUP:2 * GROUP], cos, sin)
    for h in range(HEADS):
        k_o[h] = k[:, h * HEAD_DIM:(h + 1) * HEAD_DIM].astype(BF16)
    km_o[0] = jnp.mean(k, axis=0, keepdims=True)
    vt_o[...] = p_ref[:, 2 * GROUP:3 * GROUP].T.astype(BF16)


def _rope_moba(p, cos, sin, seq):
    m = p.shape[0]
    tm = MOBA_BLOCK
    nt = seq // tm
    tab = pl.BlockSpec((tm, GROUP), lambda i: (i % nt, 0))
    col = pl.BlockSpec((GROUP, tm), lambda i: (0, i))
    return pl.pallas_call(
        _rope_moba_kernel,
        grid=(m // tm,),
        in_specs=[pl.BlockSpec((tm, 3 * GROUP), lambda i: (i, 0)), tab, tab],
        out_specs=[col, pl.BlockSpec((HEADS, tm, HEAD_DIM), lambda i: (0, i, 0)), col,
                   pl.BlockSpec((1, 1, GROUP), lambda i: (i, 0, 0))],
        out_shape=[jax.ShapeDtypeStruct((GROUP, m), F32), jax.ShapeDtypeStruct((HEADS, m, HEAD_DIM), BF16),
                   jax.ShapeDtypeStruct((GROUP, m), BF16), jax.ShapeDtypeStruct((m // tm, 1, GROUP), F32)],
        compiler_params=_cparams("arbitrary"),
        name="rope_moba",
    )(p, cos, sin)


def _rope_tables(seq):
    inv = ROPE_THETA ** (-jnp.arange(0, HEAD_DIM, 2, dtype=F32) / HEAD_DIM)
    ang = jnp.arange(seq, dtype=F32)[:, None] * inv[None, :]
    cos, sin = jnp.cos(ang), jnp.sin(ang)
    cos_full = jnp.tile(jnp.concatenate([cos, cos], axis=1), (1, HEADS))
    sin_signed = jnp.tile(jnp.concatenate([-sin, sin], axis=1), (1, HEADS))
    return cos_full, sin_signed


def _moba_kernel(qt_ref, k_ref, vt_ref, km_ref, o_ref, m_s, l_s, acc_s, s_s, p_s):
    blk = pl.program_id(1)
    tq = qt_ref.shape[1]
    n_blocks = km_ref.shape[0]
    key_i = lax.broadcasted_iota(jnp.int32, (MOBA_BLOCK, tq), 0)
    qry_i = lax.broadcasted_iota(jnp.int32, (MOBA_BLOCK, tq), 1)
    blk_id = lax.broadcasted_iota(jnp.int32, (n_blocks, tq), 0)
    valid = blk_id < blk
    heads = [slice(h * HEAD_DIM, (h + 1) * HEAD_DIM) for h in range(HEADS)]

    sels = []
    for hs in heads:
        gate = _hdot(km_ref[:, hs], qt_ref[hs, :])
        sel = jnp.zeros((n_blocks, tq), F32)
        for n in range(n_blocks):
            gn = gate[n:n + 1, :]
            beats = valid & ((gate > gn) | ((gate == gn) & (blk_id < n)))
            cnt = jnp.sum(beats.astype(F32), axis=0, keepdims=True)
            sel = jnp.where((blk_id == n) & (cnt < MOBA_TOPK) & valid, 1.0, sel)
        sels.append(sel)

    def attend(start, keep_fn, first):
        for h, hs in enumerate(heads):
            s_s[h] = jnp.dot(k_ref[h, pl.ds(start, MOBA_BLOCK), :], qt_ref[hs, :].astype(BF16),
                             preferred_element_type=F32)
        alphas = []
        for h in range(HEADS):
            s = jnp.where(keep_fn(h), s_s[h], NEG_INF)
            m_blk = jnp.max(s, axis=0, keepdims=True)
            if first:
                m_new = m_blk
            else:
                m_new = jnp.maximum(m_s[h], m_blk)
                alphas.append(jnp.exp(m_s[h] - m_new))
            p = jnp.exp(s - m_new)
            p_s[h] = p.astype(BF16)
            p_sum = jnp.sum(p, axis=0, keepdims=True)
            l_s[h] = p_sum if first else alphas[h] * l_s[h] + p_sum
            m_s[h] = m_new
        for h, hs in enumerate(heads):
            pv = jnp.dot(vt_ref[hs, pl.ds(start, MOBA_BLOCK)], p_s[h], preferred_element_type=F32)
            acc_s[hs, :] = pv if first else alphas[h] * acc_s[hs, :] + pv

    causal = key_i <= qry_i
    attend(pl.multiple_of(blk * MOBA_BLOCK, MOBA_BLOCK), lambda h: causal, True)

    def body(n, carry):
        picked = lambda h: jnp.sum(jnp.where(blk_id == n, sels[h], 0.0), axis=0, keepdims=True) > 0.5
        attend(pl.multiple_of(n * MOBA_BLOCK, MOBA_BLOCK), picked, False)
        return carry

    lax.fori_loop(0, blk, body, 0)
    out_t = jnp.concatenate([acc_s[hs, :] / l_s[h] for h, hs in enumerate(heads)], axis=0)
    o_ref[...] = out_t.T


def _moba(qt, kh, vt, kmean, nb, seq):
    m = qt.shape[1]
    nblk = seq // MOBA_BLOCK
    return pl.pallas_call(
        _moba_kernel,
        grid=(nb, nblk),
        in_specs=[pl.BlockSpec((GROUP, MOBA_BLOCK), lambda b, j: (0, b * nblk + j)),
                  pl.BlockSpec((HEADS, seq, HEAD_DIM), lambda b, j: (0, b, 0)),
                  pl.BlockSpec((GROUP, seq), lambda b, j: (0, b)),
                  pl.BlockSpec((None, nblk, GROUP), lambda b, j: (b, 0, 0))],
        out_specs=pl.BlockSpec((MOBA_BLOCK, GROUP), lambda b, j: (b * nblk + j, 0)),
        out_shape=jax.ShapeDtypeStruct((m, GROUP), F32),
        scratch_shapes=[pltpu.VMEM((HEADS, 1, MOBA_BLOCK), F32), pltpu.VMEM((HEADS, 1, MOBA_BLOCK), F32),
                        pltpu.VMEM((GROUP, MOBA_BLOCK), F32),
                        pltpu.VMEM((HEADS, MOBA_BLOCK, MOBA_BLOCK), F32),
                        pltpu.VMEM((HEADS, MOBA_BLOCK, MOBA_BLOCK), BF16)],
        compiler_params=_cparams("arbitrary", "arbitrary"),
        name="moba",
    )(qt, kh, vt, kmean.reshape(nb, nblk, GROUP))


def _ret_kernel(p_ref, cos_ref, sin_ref, gn_ref, o_ref, st_ref, mask_s):
    lc = p_ref.shape[0]
    log_gammas = [math.log1p(-(2.0 ** (-5.0 - h))) for h in range(HEADS)]

    @pl.when(pl.program_id(1) == 0)
    def _():
        st_ref[...] = jnp.zeros_like(st_ref)

    @pl.when((pl.program_id(0) == 0) & (pl.program_id(1) == 0))
    def _():
        ri = lax.broadcasted_iota(jnp.int32, (lc, lc), 0)
        ci = lax.broadcasted_iota(jnp.int32, (lc, lc), 1)
        diff = (ri - ci).astype(F32)
        for h in range(HEADS):
            mask_s[h] = jnp.where(diff >= 0, jnp.exp(log_gammas[h] * jnp.maximum(diff, 0.0)), 0.0)

    cos = cos_ref[...]
    sin = sin_ref[...]
    q_all = _rope_rows(p_ref[:, 0:GROUP], cos, sin)
    k_all = _rope_rows(p_ref[:, GROUP:2 * GROUP], cos, sin) * HEAD_DIM ** -0.5
    pos = lax.broadcasted_iota(jnp.int32, (lc, 1), 0).astype(F32)
    for h in range(HEADS):
        hs = slice(h * HEAD_DIM, (h + 1) * HEAD_DIM)
        log_gamma = log_gammas[h]
        q = q_all[:, hs]
        k = k_all[:, hs]
        v = p_ref[:, 2 * GROUP + h * HEAD_DIM:2 * GROUP + (h + 1) * HEAD_DIM]
        g = p_ref[:, 3 * GROUP + h * HEAD_DIM:3 * GROUP + (h + 1) * HEAD_DIM]
        intra = _bdot_t(q, k) * mask_s[h]
        state = st_ref[h]
        out = _bdot(intra, v) + _bdot(q * jnp.exp(log_gamma * (pos + 1.0)), state)
        k_w = jnp.exp(log_gamma * (lc - 1.0 - pos))
        kv = lax.dot_general((k * k_w).astype(BF16), v.astype(BF16), (((0,), (0,)), ((), ())),
                             preferred_element_type=F32)
        st_ref[h] = state * math.exp(log_gamma * lc) + kv
        mu = jnp.mean(out, axis=-1, keepdims=True)
        d = out - mu
        var = jnp.mean(d * d, axis=-1, keepdims=True)
        yn = d * lax.rsqrt(var + GN_EPS) * gn_ref[0:1, hs] + gn_ref[1:2, hs]
        o_ref[:, hs] = g * jax.nn.sigmoid(g) * yn


def _retention(p_ret, cos, sin, gn, nb, seq):
    m = p_ret.shape[0]
    lc = RET_CHUNK
    nt = seq // lc
    tab = pl.BlockSpec((lc, GROUP), lambda b, j: (j, 0))
    return pl.pallas_call(
        _ret_kernel,
        grid=(nb, nt),
        in_specs=[pl.BlockSpec((lc, 4 * GROUP), lambda b, j: (b * nt + j, 0)), tab, tab,
                  pl.BlockSpec(gn.shape, lambda b, j: (0, 0))],
        out_specs=pl.BlockSpec((lc, GROUP), lambda b, j: (b * nt + j, 0)),
        out_shape=jax.ShapeDtypeStruct((m, GROUP), F32),
        scratch_shapes=[pltpu.VMEM((HEADS, HEAD_DIM, HEAD_DIM), F32), pltpu.VMEM((HEADS, lc, lc), F32)],
        compiler_params=_cparams("arbitrary", "arbitrary"),
        name="retention",
    )(p_ret, cos, sin, gn)


def _out_proj_kernel(ya_ref, yb_ref, yc_ref, yd_ref, x_ref, w_ref, ln_ref, o_ref, *, alpha):
    acc = alpha * x_ref[...]
    for n, y_ref in enumerate((ya_ref, yb_ref, yc_ref, yd_ref)):
        acc = acc + _bdot(y_ref[...], w_ref[n * GROUP:(n + 1) * GROUP, :])
    o_ref[...] = _layer_norm(acc, ln_ref[0:1, :], ln_ref[1:2, :])


def _out_proj(ys, x2d, w_bf16, ln, alpha, tm=512):
    m = x2d.shape[0]
    yspec = pl.BlockSpec((tm, GROUP), lambda i: (i, 0))
    xspec = pl.BlockSpec((tm, D_MODEL), lambda i: (i, 0))
    full = lambda a: pl.BlockSpec(a.shape, lambda i: (0,) * a.ndim)
    return pl.pallas_call(
        functools.partial(_out_proj_kernel, alpha=alpha),
        grid=(m // tm,),
        in_specs=[yspec] * 4 + [xspec, full(w_bf16), full(ln)],
        out_specs=xspec,
        out_shape=jax.ShapeDtypeStruct((m, D_MODEL), F32),
        compiler_params=_cparams("arbitrary"),
        name="out_proj_ln",
    )(*ys, x2d, w_bf16, ln)


FFN_TN = 256


def _ffn_up_kernel(x_ref, halo_ref, w_ref, cw_ref, cb_ref, o_ref, *, tiles_per_seq):
    i = pl.program_id(0)
    xb = x_ref[...].astype(BF16)
    hb = halo_ref[...].astype(BF16)
    first = i % tiles_per_seq == 0

    def conv_cols(c0):
        w = w_ref[:, c0:c0 + FFN_TN]
        h = jnp.dot(xb, w, preferred_element_type=F32)
        hh = jnp.where(first, 0.0, jnp.dot(hb, w, preferred_element_type=F32))
        u = h * cw_ref[2:3, c0:c0 + FFN_TN] + cb_ref[:, c0:c0 + FFN_TN]
        for s in (1, 2):
            u = u + _shift_rows(h, hh, s) * cw_ref[2 - s:3 - s, c0:c0 + FFN_TN]
        return u

    for c in range(D_FF // FFN_TN):
        gate = conv_cols(c * FFN_TN)
        val = conv_cols(D_FF + c * FFN_TN)
        o_ref[:, c * FFN_TN:(c + 1) * FFN_TN] = (_gelu_tanh(gate) * val).astype(BF16)


def _ffn_up(x2d, w_bf16, cw, cb, seq, tm=512):
    m = x2d.shape[0]
    full = lambda a: pl.BlockSpec(a.shape, lambda i: (0,) * a.ndim)
    return pl.pallas_call(
        functools.partial(_ffn_up_kernel, tiles_per_seq=seq // tm),
        grid=(m // tm,),
        in_specs=[pl.BlockSpec((tm, D_MODEL), lambda i: (i, 0)),
                  pl.BlockSpec((SUBLANES, D_MODEL), _halo_map(tm)),
                  full(w_bf16), full(cw), full(cb)],
        out_specs=pl.BlockSpec((tm, D_FF), lambda i: (i, 0)),
        out_shape=jax.ShapeDtypeStruct((m, D_FF), BF16),
        compiler_params=_cparams("arbitrary"),
        name="ffn_up",
    )(x2d, x2d, w_bf16, cw, cb)


def _ffn_down_kernel(a_ref, x_ref, w_ref, ln_ref, o_ref, *, alpha):
    acc = alpha * x_ref[...] + jnp.dot(a_ref[...], w_ref[...], preferred_element_type=F32)
    o_ref[...] = _layer_norm(acc, ln_ref[0:1, :], ln_ref[1:2, :])


def _ffn_down(act, x2d, w_bf16, ln, alpha, tm=512):
    m = x2d.shape[0]
    xspec = pl.BlockSpec((tm, D_MODEL), lambda i: (i, 0))
    full = lambda a: pl.BlockSpec(a.shape, lambda i: (0,) * a.ndim)
    return pl.pallas_call(
        functools.partial(_ffn_down_kernel, alpha=alpha),
        grid=(m // tm,),
        in_specs=[pl.BlockSpec((tm, D_FF), lambda i: (i, 0)), xspec, full(w_bf16), full(ln)],
        out_specs=xspec,
        out_shape=jax.ShapeDtypeStruct((m, D_MODEL), F32),
        compiler_params=_cparams("arbitrary"),
        name="ffn_down_ln",
    )(act, x2d, w_bf16, ln)


def _block_diag(w):
    h, n, _ = w.shape
    eye = jnp.eye(h, dtype=w.dtype)
    return (eye[:, None, :, None] * w[:, :, None, :]).reshape(h * n, h * n)


def _pad_rows(w, offset, total):
    return jnp.zeros((total, w.shape[1]), w.dtype).at[offset:offset + w.shape[0]].set(w)


def _rows8(*vecs):
    rows = [v.reshape(1, -1) for v in vecs]
    rows += [jnp.zeros_like(rows[0])] * (SUBLANES - len(rows))
    return jnp.concatenate(rows, axis=0)


def kernel(x, w_in, tshift_mu, rwkv_w0, rwkv_w_up, rwkv_a0, rwkv_a_up, rwkv_g_up, rwkv_k_k, rwkv_k_a, rwkv_r_k, rwkv_gn_g, rwkv_gn_b, lru_conv_w, lru_conv_b, lru_w_r, lru_b_r, lru_w_i, lru_b_i, lru_lambda, ret_gn_g, ret_gn_b, w_out, ln1_g, ln1_b, ffn_w_up, ffn_conv_w, ffn_conv_b, ffn_w_down, ln2_g, ln2_b):
    nb, seq, _ = x.shape
    depth = w_in.shape[0]
    alpha = (2.0 * depth) ** 0.25
    cos, sin = _rope_tables(seq)
    ones_blk = _block_diag(jnp.ones((HEADS, HEAD_DIM, HEAD_DIM), F32))
    x2d = x.reshape(nb * seq, D_MODEL)
    for l in range(depth):
        p_rwkv, p_lru, p_att, p_ret = _in_proj(x2d, w_in[l].astype(BF16))

        lrw = jnp.stack([_pad_rows(rwkv_w_up[l], 0, RWKV_LR), _pad_rows(rwkv_a_up[l], 32, RWKV_LR),
                         _pad_rows(rwkv_g_up[l], 64, RWKV_LR)])
        pv = _rows8(rwkv_w0[l], rwkv_a0[l], rwkv_k_k[l], rwkv_k_a[l], rwkv_r_k[l])
        *scan_ops, g, bv = _rwkv_prep(p_rwkv.reshape(nb, seq, RWKV_COLS), tshift_mu[l].reshape(1, -1), pv, lrw,
                                      ones_blk, nb, seq)
        y = _rwkv_scan(*scan_ops, nb)
        y_a = _rwkv_post(y, g, bv, _rows8(rwkv_gn_g[l], rwkv_gn_b[l]), ones_blk, nb, seq)
        y_a = y_a.reshape(nb * seq, GROUP)

        y_b = _lru(p_lru, lru_conv_w[l], _rows8(lru_conv_b[l], lru_b_r[l], lru_b_i[l], lru_lambda[l]),
                   _block_diag(lru_w_r[l]).astype(BF16), _block_diag(lru_w_i[l]).astype(BF16), nb, seq)

        y_c = _moba(*_rope_moba(p_att, cos, sin, seq), nb, seq)

        y_d = _retention(p_ret, cos, sin, _rows8(ret_gn_g[l], ret_gn_b[l]), nb, seq)

        x2d = _out_proj((y_a, y_b, y_c, y_d), x2d, w_out[l].astype(BF16), _rows8(ln1_g[l], ln1_b[l]), alpha)
        act = _ffn_up(x2d, ffn_w_up[l].astype(BF16), ffn_conv_w[l], ffn_conv_b[l].reshape(1, -1), seq)
        x2d = _ffn_down(act, x2d, ffn_w_down[l].astype(BF16), _rows8(ln2_g[l], ln2_b[l]), alpha)
    return x2d.reshape(nb, seq, D_MODEL)
```

```python
import functools
import math

import jax
import jax.numpy as jnp
from jax import lax
from jax.experimental import pallas as pl
from jax.experimental.pallas import tpu as pltpu

F32 = jnp.float32
BF16 = jnp.bfloat16
HIGHEST = lax.Precision.HIGHEST

D_MODEL = 1024
HEAD_DIM = 64
GROUP = 256
HEADS = GROUP // HEAD_DIM
RWKV_COLS = 3 * GROUP + 32 + 32 + 64
RWKV_LR = RWKV_COLS - 3 * GROUP
RWKV_GN_EPS = 64e-5
LRU_C = 8.0
MOBA_BLOCK = 256
MOBA_TOPK = 3
ROPE_THETA = 10000.0
NEG_INF = -1e30
GN_EPS = 1e-5
D_FF = 2816
LN_EPS = 1e-5
RET_CHUNK = 256

VMEM_LIMIT_BYTES = 56 * 1024 * 1024
SUBLANES = 8
LANES = 128


def _cparams(*semantics):
    return pltpu.CompilerParams(dimension_semantics=semantics, vmem_limit_bytes=VMEM_LIMIT_BYTES)


def _hdot(a, b):
    return jnp.dot(a, b, precision=HIGHEST, preferred_element_type=F32)


def _bdot(a, b):
    return jnp.dot(a.astype(BF16), b.astype(BF16), preferred_element_type=F32)


def _bdot_t(a, b):
    return lax.dot_general(a.astype(BF16), b.astype(BF16), (((1,), (1,)), ((), ())),
                           preferred_element_type=F32)


def _softplus(z):
    return jnp.maximum(z, 0.0) + jnp.log1p(jnp.exp(-jnp.abs(z)))


def _gelu_tanh(x):
    return 0.5 * x * (1.0 + jnp.tanh(math.sqrt(2.0 / math.pi) * (x + 0.044715 * (x * x * x))))


def _shift_rows(x, halo, s):
    rolled = pltpu.roll(x, s, axis=0)
    row8 = lax.broadcasted_iota(jnp.int32, halo.shape, 0)
    top = jnp.where(row8 < s, pltpu.roll(halo, s, axis=0), rolled[0:SUBLANES])
    return jnp.concatenate([top, rolled[SUBLANES:]], axis=0)


def _layer_norm(y, g, b):
    mu = jnp.mean(y, axis=-1, keepdims=True)
    d = y - mu
    var = jnp.mean(d * d, axis=-1, keepdims=True)
    return d * lax.rsqrt(var + LN_EPS) * g + b


def _halo_map(tm, rows=SUBLANES):
    return lambda i: (jnp.maximum(i * (tm // rows) - 1, 0), 0)


def _split3(x):
    hi = x.astype(BF16)
    r = x - hi.astype(F32)
    mid = r.astype(BF16)
    lo = (r - mid.astype(F32)).astype(BF16)
    return hi, mid, lo


def _seg_sum(x, ones_blk):
    return sum(jnp.dot(part, ones_blk, preferred_element_type=F32) for part in _split3(x))


def _dot3(a, b):
    a_hi, a_lo, _ = _split3(a)
    b_hi, b_lo, _ = _split3(b)
    dot = functools.partial(jnp.dot, preferred_element_type=F32)
    return dot(a_hi, b_hi) + (dot(a_hi, b_lo) + dot(a_lo, b_hi))


IN_WIDTHS = (RWKV_COLS, 2 * GROUP, 3 * GROUP, 4 * GROUP)
IN_COLS = sum(IN_WIDTHS)


def _in_proj_kernel(x_ref, w_ref, *outs):
    xb = x_ref[...].astype(BF16)
    col = 0
    for o in outs:
        n = o.shape[1]
        o[...] = jnp.dot(xb, w_ref[:, col:col + n], preferred_element_type=F32)
        col += n


def _in_proj(x2d, w_bf16, tm=512):
    m = x2d.shape[0]
    return pl.pallas_call(
        _in_proj_kernel,
        grid=(m // tm,),
        in_specs=[pl.BlockSpec((tm, D_MODEL), lambda i: (i, 0)),
                  pl.BlockSpec((D_MODEL, IN_COLS), lambda i: (0, 0))],
        out_specs=[pl.BlockSpec((tm, w), lambda i: (i, 0)) for w in IN_WIDTHS],
        out_shape=[jax.ShapeDtypeStruct((m, w), F32) for w in IN_WIDTHS],
        compiler_params=_cparams("arbitrary"),
        name="in_proj",
    )(x2d, w_bf16)


def _rows_to_lanes(x_s, out_ref, t0, nb, tb, gw, dup):
    steps = LANES // gw
    lane_grp = lax.broadcasted_iota(jnp.int32, (nb, LANES), 1) // gw
    z = [None] * (GROUP // gw)
    for tt in range(steps):
        xt = [x_s[c, pl.ds(t0 + tt, nb, stride=tb), :] for c in range(GROUP // LANES)]
        for g in range(len(z)):
            half = xt[g // steps]
            shift = (tt - g % steps) % steps * gw
            piece = pltpu.roll(half, shift, axis=1) if shift else half
            z[g] = piece if tt == 0 else jnp.where(lane_grp == tt, piece, z[g])
    tile = jnp.concatenate([zg for zg in z for _ in range(dup)], axis=0)
    out_ref[pl.ds(t0, steps)] = tile.T.reshape(steps, gw, tile.shape[0])


def _lanes_to_rows(y_ref, y_s, t0, nb, tb):
    steps = LANES // HEAD_DIM
    tile = y_ref[pl.ds(t0, steps)]
    tile = tile.reshape(steps * HEAD_DIM, tile.shape[2]).T
    lane_grp = lax.broadcasted_iota(jnp.int32, (nb, LANES), 1) // HEAD_DIM
    for tt in range(steps):
        for half in range(GROUP // LANES):
            cols = None
            for pos in range(steps):
                h = half * steps + pos
                piece = tile[h * 2 * nb:h * 2 * nb + nb, :]
                shift = (pos - tt) % steps * HEAD_DIM
                piece = pltpu.roll(piece, shift, axis=1) if shift else piece
                cols = piece if pos == 0 else jnp.where(lane_grp == pos, piece, cols)
            y_s[half, pl.ds(t0 + tt, nb, stride=tb), :] = cols


def _rwkv_prep_kernel(p_ref, halo_ref, mu_ref, pv_ref, lrw_ref, ones_ref,
                      r_o, k_o, v_o, kk_o, b_o, w_o, g_o, bv_o, *stage, nb, tb):
    rows = nb * tb
    pa = p_ref[...].reshape(rows, RWKV_COLS)
    prev = jnp.where(pl.program_id(0) == 0, 0.0, halo_ref[:, SUBLANES - 1:SUBLANES, :])
    prev = jnp.broadcast_to(prev, (nb, tb, RWKV_COLS)).reshape(rows, RWKV_COLS)
    t_idx = lax.broadcasted_iota(jnp.int32, pa.shape, 0) % tb
    shifted = jnp.where(t_idx == 0, prev, pltpu.roll(pa, 1, axis=0))
    pa = pa + (shifted - pa) * mu_ref[...]
    r = pa[:, 0:GROUP]
    k = pa[:, GROUP:2 * GROUP]
    v = pa[:, 2 * GROUP:3 * GROUP]
    lr = pa[:, 3 * GROUP:RWKV_COLS]
    w0, a0, k_k, k_a, r_k = (pv_ref[n:n + 1, :] for n in range(5))
    tw = _dot3(jnp.tanh(lr), lrw_ref[0])
    ta = _dot3(lr, lrw_ref[1])
    g = _dot3(jax.nn.sigmoid(lr), lrw_ref[2])
    w_log = -_softplus(-(w0 + tw)) - 0.5
    decay = jnp.exp(-jnp.exp(w_log))
    a = jax.nn.sigmoid(a0 + ta)
    kk = k * k_k
    k2 = k * (1.0 + (a - 1.0) * k_a)
    ss = _seg_sum(kk * kk, ones_ref[...])
    kkn = kk / jnp.maximum(jnp.sqrt(ss), 1e-12)
    bonus = _seg_sum(r * k2 * r_k, ones_ref[...])
    g_o[...] = g.reshape(nb, tb, GROUP)
    bv_o[...] = (bonus * v).reshape(nb, tb, GROUP)
    layouts = ((r, r_o, HEAD_DIM // 2, 1), (k2, k_o, HEAD_DIM // 2, 1), (v, v_o, HEAD_DIM, 2),
               (kkn, kk_o, HEAD_DIM // 2, 1), (kkn * a, b_o, HEAD_DIM // 2, 1), (decay, w_o, HEAD_DIM // 2, 1))
    for (x, _, _, _), x_s in zip(layouts, stage):
        for c in range(GROUP // LANES):
            x_s[c] = x[:, c * LANES:(c + 1) * LANES]
    span = LANES // (HEAD_DIM // 2)

    def move(n, carry):
        for (_, out, gw, dup), x_s in zip(layouts, stage):
            for t0 in range(0, span, LANES // gw):
                _rows_to_lanes(x_s, out, n * span + t0, nb, tb, gw, dup)
        return carry

    lax.fori_loop(0, tb // span, move, 0)


RWKV_TB = 32


def _rwkv_prep(p_rwkv, mu, pv, lrw, ones_blk, nb, seq, tb=RWKV_TB):
    lanes = 2 * HEADS * nb
    full = lambda a: pl.BlockSpec(a.shape, lambda i: (0,) * a.ndim)
    key_spec = pl.BlockSpec((tb, HEAD_DIM // 2, lanes), lambda i: (i, 0, 0))
    val_spec = pl.BlockSpec((tb, HEAD_DIM, lanes), lambda i: (i, 0, 0))
    nat_spec = pl.BlockSpec((nb, tb, GROUP), lambda i: (0, i, 0))
    key_shape = jax.ShapeDtypeStruct((seq, HEAD_DIM // 2, lanes), F32)
    val_shape = jax.ShapeDtypeStruct((seq, HEAD_DIM, lanes), F32)
    nat_shape = jax.ShapeDtypeStruct((nb, seq, GROUP), F32)
    return pl.pallas_call(
        functools.partial(_rwkv_prep_kernel, nb=nb, tb=tb),
        grid=(seq // tb,),
        in_specs=[pl.BlockSpec((nb, tb, RWKV_COLS), lambda i: (0, i, 0)),
                  pl.BlockSpec((nb, SUBLANES, RWKV_COLS),
                               lambda i: (0, jnp.maximum(i * (tb // SUBLANES) - 1, 0), 0)),
                  full(mu), full(pv), full(lrw), full(ones_blk)],
        out_specs=[key_spec, key_spec, val_spec, key_spec, key_spec, key_spec, nat_spec, nat_spec],
        out_shape=[key_shape, key_shape, val_shape, key_shape, key_shape, key_shape, nat_shape, nat_shape],
        scratch_shapes=[pltpu.VMEM((GROUP // LANES, nb * tb, LANES), F32)] * 6,
        compiler_params=_cparams("arbitrary"),
        name="rwkv_prep",
    )(p_rwkv, p_rwkv, mu, pv, lrw, ones_blk)


def _rwkv_scan_kernel(r_ref, k_ref, v_ref, kk_ref, b_ref, w_ref, y_ref, s_ref, *, tb, nb):
    n_jlo = s_ref.shape[0]
    lanes = s_ref.shape[2]

    @pl.when(pl.program_id(0) == 0)
    def _():
        s_ref[...] = jnp.zeros_like(s_ref)

    lane = lax.broadcasted_iota(jnp.int32, (HEAD_DIM, lanes), 1)
    upper = (lane // nb) % 2 == 1

    def both_halves(x):
        return x + jnp.where(upper, pltpu.roll(x, nb, axis=1), pltpu.roll(x, lanes - nb, axis=1))

    acc = jnp.zeros((HEAD_DIM, lanes), F32)
    for j in range(n_jlo):
        acc = acc + s_ref[j] * kk_ref[0, pl.ds(j, 1), :]

    def step(t, sa):
        t_next = jnp.minimum(t + 1, tb - 1)
        vt = v_ref[t]
        yacc = jnp.zeros((HEAD_DIM, lanes), F32)
        nacc = jnp.zeros((HEAD_DIM, lanes), F32)
        for j in range(n_jlo):
            sj = (s_ref[j] * w_ref[t, pl.ds(j, 1), :] - sa * b_ref[t, pl.ds(j, 1), :]
                  + vt * k_ref[t, pl.ds(j, 1), :])
            s_ref[j] = sj
            yacc = yacc + sj * r_ref[t, pl.ds(j, 1), :]
            nacc = nacc + sj * kk_ref[t_next, pl.ds(j, 1), :]
        y_ref[t] = both_halves(yacc)
        return both_halves(nacc)

    lax.fori_loop(0, tb, step, both_halves(acc))


def _rwkv_scan(r, k, v, kk, b, w, nb, tb=64):
    seq, n_jlo, lanes = r.shape
    kspec = pl.BlockSpec((tb, n_jlo, lanes), lambda i: (i, 0, 0))
    vspec = pl.BlockSpec((tb, HEAD_DIM, lanes), lambda i: (i, 0, 0))
    return pl.pallas_call(
        functools.partial(_rwkv_scan_kernel, tb=tb, nb=nb),
        grid=(seq // tb,),
        in_specs=[kspec, kspec, vspec, kspec, kspec, kspec],
        out_specs=vspec,
        out_shape=jax.ShapeDtypeStruct((seq, HEAD_DIM, lanes), F32),
        scratch_shapes=[pltpu.VMEM((n_jlo, HEAD_DIM, lanes), F32)],
        compiler_params=_cparams("arbitrary"),
        name="rwkv_scan",
    )(r, k, v, kk, b, w)


def _rwkv_post_kernel(y_ref, g_ref, bv_ref, gn_ref, ones_ref, o_ref, y_s, *, nb, tb):
    steps = LANES // HEAD_DIM
    unroll = 4

    def move(n, carry):
        for u in range(unroll):
            _lanes_to_rows(y_ref, y_s, (n * unroll + u) * steps, nb, tb)
        return carry

    lax.fori_loop(0, tb // (steps * unroll), move, 0)
    y = jnp.concatenate([y_s[c] for c in range(GROUP // LANES)], axis=1)
    inv_n = 1.0 / HEAD_DIM
    mu = _seg_sum(y, ones_ref[...]) * inv_n
    d = y - mu
    var = _seg_sum(d * d, ones_ref[...]) * inv_n
    yn = d * lax.rsqrt(var + RWKV_GN_EPS) * gn_ref[0:1, :] + gn_ref[1:2, :]
    out = (yn + bv_ref[...].reshape(nb * tb, GROUP)) * g_ref[...].reshape(nb * tb, GROUP)
    o_ref[...] = out.reshape(nb, tb, GROUP)


def _rwkv_post(y, g, bv, gn, ones_blk, nb, seq, tb=RWKV_TB):
    nat_spec = pl.BlockSpec((nb, tb, GROUP), lambda i: (0, i, 0))
    full = lambda a: pl.BlockSpec(a.shape, lambda i: (0,) * a.ndim)
    return pl.pallas_call(
        functools.partial(_rwkv_post_kernel, nb=nb, tb=tb),
        grid=(seq // tb,),
        in_specs=[pl.BlockSpec((tb, HEAD_DIM, y.shape[2]), lambda i: (i, 0, 0)), nat_spec, nat_spec,
                  full(gn), full(ones_blk)],
        out_specs=nat_spec,
        out_shape=jax.ShapeDtypeStruct((nb, seq, GROUP), F32),
        scratch_shapes=[pltpu.VMEM((GROUP // LANES, nb * tb, LANES), F32)],
        compiler_params=_cparams("arbitrary"),
        name="rwkv_post",
    )(y, g, bv, gn, ones_blk)


def _lru_kernel(p_ref, halo_ref, cw_ref, pv_ref, wr_ref, wi_ref, o_ref, hc_ref, *, tt):
    j = pl.program_id(1)
    x = p_ref[:, 0:GROUP]
    gb = p_ref[:, GROUP:2 * GROUP]
    halo = jnp.where(j == 0, 0.0, halo_ref[:, 0:GROUP])
    conv_b, b_r, b_i, lam = (pv_ref[n:n + 1, :] for n in range(4))
    xc = x * cw_ref[3:4, :] + conv_b
    for s in (1, 2, 3):
        xc = xc + _shift_rows(x, halo, s) * cw_ref[3 - s:4 - s, :]
    gate_r = jax.nn.sigmoid(_bdot(xc, wr_ref[...]) + b_r)
    gate_i = jax.nn.sigmoid(_bdot(xc, wi_ref[...]) + b_i)
    log_a = -LRU_C * gate_r * _softplus(-lam)
    z = 2.0 * log_a
    expm1_z = jnp.tanh(0.5 * z) * (jnp.exp(z) + 1.0)
    a = jnp.exp(log_a)
    u = jnp.sqrt(-expm1_z) * (gate_i * xc)

    @pl.when(j == 0)
    def _():
        hc_ref[...] = jnp.zeros_like(hc_ref)

    row = lax.broadcasted_iota(jnp.int32, a.shape, 0)
    d = 1
    while d < tt:
        keep = row >= d
        u = a * jnp.where(keep, pltpu.roll(u, d, axis=0), 0.0) + u
        a = a * jnp.where(keep, pltpu.roll(a, d, axis=0), 1.0)
        d *= 2
    h = a * hc_ref[...] + u
    hc_ref[...] = h[tt - 1:tt, :]
    o_ref[...] = _gelu_tanh(gb) * h


def _lru(p_lru, cw, pv, wr, wi, nb, seq, tt=256):
    m = p_lru.shape[0]
    nt = seq // tt
    full = lambda a: pl.BlockSpec(a.shape, lambda b, j: (0,) * a.ndim)
    halo = _halo_map(tt)
    return pl.pallas_call(
        functools.partial(_lru_kernel, tt=tt),
        grid=(nb, nt),
        in_specs=[pl.BlockSpec((tt, 2 * GROUP), lambda b, j: (b * nt + j, 0)),
                  pl.BlockSpec((SUBLANES, 2 * GROUP), lambda b, j: halo(b * nt + j)),
                  full(cw), full(pv), full(wr), full(wi)],
        out_specs=pl.BlockSpec((tt, GROUP), lambda b, j: (b * nt + j, 0)),
        out_shape=jax.ShapeDtypeStruct((m, GROUP), F32),
        scratch_shapes=[pltpu.VMEM((1, GROUP), F32)],
        compiler_params=_cparams("arbitrary", "arbitrary"),
        name="lru",
    )(p_lru, p_lru, cw, pv, wr, wi)


def _rope_rows(x, cos, sin_signed):
    lane = lax.broadcasted_iota(jnp.int32, x.shape, 1)
    first_half = lane % HEAD_DIM < HEAD_DIM // 2
    width = x.shape[1]
    partner = jnp.where(first_half, pltpu.roll(x, width - HEAD_DIM // 2, axis=1),
                        pltpu.roll(x, HEAD_DIM // 2, axis=1))
    return x * cos + partner * sin_signed


def _rope_moba_kernel(p_ref, cos_ref, sin_ref, qt_o, k_o, vt_o, km_o):
    cos = cos_ref[...]
    sin = sin_ref[...]
    q = _rope_rows(p_ref[:, 0:GROUP], cos, sin) * HEAD_DIM ** -0.5
    qt_o[...] = q.T
    k = _rope_rows(p_ref[:, GROUP:2 * GROUP], cos, sin)
    for h in range(HEADS):
        k_o[h] = k[:, h * HEAD_DIM:(h + 1) * HEAD_DIM].astype(BF16)
    km_o[0] = jnp.mean(k, axis=0, keepdims=True)
    vt_o[...] = p_ref[:, 2 * GROUP:3 * GROUP].T.astype(BF16)


def _rope_moba(p, cos, sin, seq):
    m = p.shape[0]
    tm = MOBA_BLOCK
    nt = seq // tm
    tab = pl.BlockSpec((tm, GROUP), lambda i: (i % nt, 0))
    col = pl.BlockSpec((GROUP, tm), lambda i: (0, i))
    return pl.pallas_call(
        _rope_moba_kernel,
        grid=(m // tm,),
        in_specs=[pl.BlockSpec((tm, 3 * GROUP), lambda i: (i, 0)), tab, tab],
        out_specs=[col, pl.BlockSpec((HEADS, tm, HEAD_DIM), lambda i: (0, i, 0)), col,
                   pl.BlockSpec((1, 1, GROUP), lambda i: (i, 0, 0))],
        out_shape=[jax.ShapeDtypeStruct((GROUP, m), F32), jax.ShapeDtypeStruct((HEADS, m, HEAD_DIM), BF16),
                   jax.ShapeDtypeStruct((GROUP, m), BF16), jax.ShapeDtypeStruct((m // tm, 1, GROUP), F32)],
        compiler_params=_cparams("arbitrary"),
        name="rope_moba",
    )(p, cos, sin)


def _rope_tables(seq):
    inv = ROPE_THETA ** (-jnp.arange(0, HEAD_DIM, 2, dtype=F32) / HEAD_DIM)
    ang = jnp.arange(seq, dtype=F32)[:, None] * inv[None, :]
    cos, sin = jnp.cos(ang), jnp.sin(ang)
    cos_full = jnp.tile(jnp.concatenate([cos, cos], axis=1), (1, HEADS))
    sin_signed = jnp.tile(jnp.concatenate([-sin, sin], axis=1), (1, HEADS))
    return cos_full, sin_signed


def _moba_kernel(qt_ref, k_ref, vt_ref, km_ref, o_ref, m_s, l_s, acc_s, s_s, p_s):
    blk = pl.program_id(1)
    tq = qt_ref.shape[1]
    n_blocks = km_ref.shape[0]
    key_i = lax.broadcasted_iota(jnp.int32, (MOBA_BLOCK, tq), 0)
    qry_i = lax.broadcasted_iota(jnp.int32, (MOBA_BLOCK, tq), 1)
    blk_id = lax.broadcasted_iota(jnp.int32, (n_blocks, tq), 0)
    valid = blk_id < blk
    heads = [slice(h * HEAD_DIM, (h + 1) * HEAD_DIM) for h in range(HEADS)]

    sels = []
    for hs in heads:
        gate = _hdot(km_ref[:, hs], qt_ref[hs, :])
        sel = jnp.zeros((n_blocks, tq), F32)
        for n in range(n_blocks):
            gn = gate[n:n + 1, :]
            beats = valid & ((gate > gn) | ((gate == gn) & (blk_id < n)))
            cnt = jnp.sum(beats.astype(F32), axis=0, keepdims=True)
            sel = jnp.where((blk_id == n) & (cnt < MOBA_TOPK) & valid, 1.0, sel)
        sels.append(sel)

    def attend(start, keep_fn, first):
        for h, hs in enumerate(heads):
            s_s[h] = jnp.dot(k_ref[h, pl.ds(start, MOBA_BLOCK), :], qt_ref[hs, :].astype(BF16),
                             preferred_element_type=F32)
        alphas = []
        for h in range(HEADS):
            s = jnp.where(keep_fn(h), s_s[h], NEG_INF)
            m_blk = jnp.max(s, axis=0, keepdims=True)
            if first:
                m_new = m_blk
            else:
                m_new = jnp.maximum(m_s[h], m_blk)
                alphas.append(jnp.exp(m_s[h] - m_new))
            p = jnp.exp(s - m_new)
            p_s[h] = p.astype(BF16)
            p_sum = jnp.sum(p, axis=0, keepdims=True)
            l_s[h] = p_sum if first else alphas[h] * l_s[h] + p_sum
            m_s[h] = m_new
        for h, hs in enumerate(heads):
            pv = jnp.dot(vt_ref[hs, pl.ds(start, MOBA_BLOCK)], p_s[h], preferred_element_type=F32)
            acc_s[hs, :] = pv if first else alphas[h] * acc_s[hs, :] + pv

    causal = key_i <= qry_i
    attend(pl.multiple_of(blk * MOBA_BLOCK, MOBA_BLOCK), lambda h: causal, True)

    def body(n, carry):
        picked = lambda h: jnp.sum(jnp.where(blk_id == n, sels[h], 0.0), axis=0, keepdims=True) > 0.5
        attend(pl.multiple_of(n * MOBA_BLOCK, MOBA_BLOCK), picked, False)
        return carry

    lax.fori_loop(0, blk, body, 0)
    out_t = jnp.concatenate([acc_s[hs, :] / l_s[h] for h, hs in enumerate(heads)], axis=0)
    o_ref[...] = out_t.T


def _moba(qt, kh, vt, kmean, nb, seq):
    m = qt.shape[1]
    nblk = seq // MOBA_BLOCK
    return pl.pallas_call(
        _moba_kernel,
        grid=(nb, nblk),
        in_specs=[pl.BlockSpec((GROUP, MOBA_BLOCK), lambda b, j: (0, b * nblk + j)),
                  pl.BlockSpec((HEADS, seq, HEAD_DIM), lambda b, j: (0, b, 0)),
                  pl.BlockSpec((GROUP, seq), lambda b, j: (0, b)),
                  pl.BlockSpec((None, nblk, GROUP), lambda b, j: (b, 0, 0))],
        out_specs=pl.BlockSpec((MOBA_BLOCK, GROUP), lambda b, j: (b * nblk + j, 0)),
        out_shape=jax.ShapeDtypeStruct((m, GROUP), F32),
        scratch_shapes=[pltpu.VMEM((HEADS, 1, MOBA_BLOCK), F32), pltpu.VMEM((HEADS, 1, MOBA_BLOCK), F32),
                        pltpu.VMEM((GROUP, MOBA_BLOCK), F32),
                        pltpu.VMEM((HEADS, MOBA_BLOCK, MOBA_BLOCK), F32),
                        pltpu.VMEM((HEADS, MOBA_BLOCK, MOBA_BLOCK), BF16)],
        compiler_params=_cparams("arbitrary", "arbitrary"),
        name="moba",
    )(qt, kh, vt, kmean.reshape(nb, nblk, GROUP))


def _ret_kernel(p_ref, cos_ref, sin_ref, gn_ref, o_ref, st_ref, mask_s):
    lc = p_ref.shape[0]
    log_gammas = [math.log1p(-(2.0 ** (-5.0 - h))) for h in range(HEADS)]

    @pl.when(pl.program_id(1) == 0)
    def _():
        st_ref[...] = jnp.zeros_like(st_ref)

    @pl.when((pl.program_id(0) == 0) & (pl.program_id(1) == 0))
    def _():
        ri = lax.broadcasted_iota(jnp.int32, (lc, lc), 0)
        ci = lax.broadcasted_iota(jnp.int32, (lc, lc), 1)
        diff = (ri - ci).astype(F32)
        for h in range(HEADS):
            mask_s[h] = jnp.where(diff >= 0, jnp.exp(log_gammas[h] * jnp.maximum(diff, 0.0)), 0.0)

    cos = cos_ref[...]
    sin = sin_ref[...]
    q_all = _rope_rows(p_ref[:, 0:GROUP], cos, sin)
    k_all = _rope_rows(p_ref[:, GROUP:2 * GROUP], cos, sin) * HEAD_DIM ** -0.5
    pos = lax.broadcasted_iota(jnp.int32, (lc, 1), 0).astype(F32)
    for h in range(HEADS):
        hs = slice(h * HEAD_DIM, (h + 1) * HEAD_DIM)
        log_gamma = log_gammas[h]
        q = q_all[:, hs]
        k = k_all[:, hs]
        v = p_ref[:, 2 * GROUP + h * HEAD_DIM:2 * GROUP + (h + 1) * HEAD_DIM]
        g = p_ref[:, 3 * GROUP + h * HEAD_DIM:3 * GROUP + (h + 1) * HEAD_DIM]
        intra = _bdot_t(q, k) * mask_s[h]
        state = st_ref[h]
        out = _bdot(intra, v) + _bdot(q * jnp.exp(log_gamma * (pos + 1.0)), state)
        k_w = jnp.exp(log_gamma * (lc - 1.0 - pos))
        kv = lax.dot_general((k * k_w).astype(BF16), v.astype(BF16), (((0,), (0,)), ((), ())),
                             preferred_element_type=F32)
        st_ref[h] = state * math.exp(log_gamma * lc) + kv
        mu = jnp.mean(out, axis=-1, keepdims=True)
        d = out - mu
        var = jnp.mean(d * d, axis=-1, keepdims=True)
        yn = d * lax.rsqrt(var + GN_EPS) * gn_ref[0:1, hs] + gn_ref[1:2, hs]
        o_ref[:, hs] = g * jax.nn.sigmoid(g) * yn


def _retention(p_ret, cos, sin, gn, nb, seq):
    m = p_ret.shape[0]
    lc = RET_CHUNK
    nt = seq // lc
    tab = pl.BlockSpec((lc, GROUP), lambda b, j: (j, 0))
    return pl.pallas_call(
        _ret_kernel,
        grid=(nb, nt),
        in_specs=[pl.BlockSpec((lc, 4 * GROUP), lambda b, j: (b * nt + j, 0)), tab, tab,
                  pl.BlockSpec(gn.shape, lambda b, j: (0, 0))],
        out_specs=pl.BlockSpec((lc, GROUP), lambda b, j: (b * nt + j, 0)),
        out_shape=jax.ShapeDtypeStruct((m, GROUP), F32),
        scratch_shapes=[pltpu.VMEM((HEADS, HEAD_DIM, HEAD_DIM), F32), pltpu.VMEM((HEADS, lc, lc), F32)],
        compiler_params=_cparams("arbitrary", "arbitrary"),
        name="retention",
    )(p_ret, cos, sin, gn)


def _out_proj_kernel(ya_ref, yb_ref, yc_ref, yd_ref, x_ref, w_ref, ln_ref, o_ref, *, alpha):
    y = jnp.concatenate([r[...].astype(BF16) for r in (ya_ref, yb_ref, yc_ref, yd_ref)], axis=1)
    acc = alpha * x_ref[...] + jnp.dot(y, w_ref[...], preferred_element_type=F32)
    o_ref[...] = _layer_norm(acc, ln_ref[0:1, :], ln_ref[1:2, :])


def _out_proj(ys, x2d, w_bf16, ln, alpha, tm=512):
    m = x2d.shape[0]
    yspec = pl.BlockSpec((tm, GROUP), lambda i: (i, 0))
    xspec = pl.BlockSpec((tm, D_MODEL), lambda i: (i, 0))
    full = lambda a: pl.BlockSpec(a.shape, lambda i: (0,) * a.ndim)
    return pl.pallas_call(
        functools.partial(_out_proj_kernel, alpha=alpha),
        grid=(m // tm,),
        in_specs=[yspec] * 4 + [xspec, full(w_bf16), full(ln)],
        out_specs=xspec,
        out_shape=jax.ShapeDtypeStruct((m, D_MODEL), F32),
        compiler_params=_cparams("arbitrary"),
        name="out_proj_ln",
    )(*ys, x2d, w_bf16, ln)


FFN_TN = 256


def _ffn_up_kernel(x_ref, w_ref, cw_ref, cb_ref, o_ref, tail_ref, h_s, *, tiles_per_seq):
    xb = x_ref[...].astype(BF16)
    tm = xb.shape[0]
    first = pl.program_id(0) % tiles_per_seq == 0

    @pl.when(pl.program_id(0) == 0)
    def _():
        tail_ref[...] = jnp.zeros_like(tail_ref)

    def conv_cols(c0, slot):
        h = jnp.dot(xb, w_ref[:, c0:c0 + FFN_TN], preferred_element_type=F32)
        h_s[slot, 0:SUBLANES, :] = jnp.where(first, 0.0, tail_ref[:, c0:c0 + FFN_TN])
        h_s[slot, SUBLANES:, :] = h
        tail_ref[:, c0:c0 + FFN_TN] = h[tm - SUBLANES:tm]
        u = h * cw_ref[2:3, c0:c0 + FFN_TN] + cb_ref[:, c0:c0 + FFN_TN]
        for s in (1, 2):
            u = u + h_s[slot, pl.ds(SUBLANES - s, tm), :] * cw_ref[2 - s:3 - s, c0:c0 + FFN_TN]
        return u

    for c in range(D_FF // FFN_TN):
        gate = conv_cols(c * FFN_TN, 0)
        val = conv_cols(D_FF + c * FFN_TN, 1)
        o_ref[:, c * FFN_TN:(c + 1) * FFN_TN] = (_gelu_tanh(gate) * val).astype(BF16)


def _ffn_up(x2d, w_bf16, cw, cb, seq, tm=512):
    m = x2d.shape[0]
    full = lambda a: pl.BlockSpec(a.shape, lambda i: (0,) * a.ndim)
    return pl.pallas_call(
        functools.partial(_ffn_up_kernel, tiles_per_seq=seq // tm),
        grid=(m // tm,),
        in_specs=[pl.BlockSpec((tm, D_MODEL), lambda i: (i, 0)), full(w_bf16), full(cw), full(cb)],
        out_specs=pl.BlockSpec((tm, D_FF), lambda i: (i, 0)),
        out_shape=jax.ShapeDtypeStruct((m, D_FF), BF16),
        scratch_shapes=[pltpu.VMEM((SUBLANES, 2 * D_FF), F32), pltpu.VMEM((2, SUBLANES + tm, FFN_TN), F32)],
        compiler_params=_cparams("arbitrary"),
        name="ffn_up",
    )(x2d, w_bf16, cw, cb)


def _ffn_down_kernel(a_ref, x_ref, w_ref, ln_ref, o_ref, *, alpha):
    acc = alpha * x_ref[...] + jnp.dot(a_ref[...], w_ref[...], preferred_element_type=F32)
    o_ref[...] = _layer_norm(acc, ln_ref[0:1, :], ln_ref[1:2, :])


def _ffn_down(act, x2d, w_bf16, ln, alpha, tm=512):
    m = x2d.shape[0]
    xspec = pl.BlockSpec((tm, D_MODEL), lambda i: (i, 0))
    full = lambda a: pl.BlockSpec(a.shape, lambda i: (0,) * a.ndim)
    return pl.pallas_call(
        functools.partial(_ffn_down_kernel, alpha=alpha),
        grid=(m // tm,),
        in_specs=[pl.BlockSpec((tm, D_FF), lambda i: (i, 0)), xspec, full(w_bf16), full(ln)],
        out_specs=xspec,
        out_shape=jax.ShapeDtypeStruct((m, D_MODEL), F32),
        compiler_params=_cparams("arbitrary"),
        name="ffn_down_ln",
    )(act, x2d, w_bf16, ln)


def _block_diag(w):
    h, n, _ = w.shape
    eye = jnp.eye(h, dtype=w.dtype)
    return (eye[:, None, :, None] * w[:, :, None, :]).reshape(h * n, h * n)


def _pad_rows(w, offset, total):
    return jnp.zeros((total, w.shape[1]), w.dtype).at[offset:offset + w.shape[0]].set(w)


def _rows8(*vecs):
    rows = [v.reshape(1, -1) for v in vecs]
    rows += [jnp.zeros_like(rows[0])] * (SUBLANES - len(rows))
    return jnp.concatenate(rows, axis=0)


def kernel(x, w_in, tshift_mu, rwkv_w0, rwkv_w_up, rwkv_a0, rwkv_a_up, rwkv_g_up, rwkv_k_k, rwkv_k_a, rwkv_r_k, rwkv_gn_g, rwkv_gn_b, lru_conv_w, lru_conv_b, lru_w_r, lru_b_r, lru_w_i, lru_b_i, lru_lambda, ret_gn_g, ret_gn_b, w_out, ln1_g, ln1_b, ffn_w_up, ffn_conv_w, ffn_conv_b, ffn_w_down, ln2_g, ln2_b):
    nb, seq, _ = x.shape
    depth = w_in.shape[0]
    alpha = (2.0 * depth) ** 0.25
    cos, sin = _rope_tables(seq)
    ones_blk = _block_diag(jnp.ones((HEADS, HEAD_DIM, HEAD_DIM), F32)).astype(BF16)
    x2d = x.reshape(nb * seq, D_MODEL)
    for l in range(depth):
        p_rwkv, p_lru, p_att, p_ret = _in_proj(x2d, w_in[l].astype(BF16))

        lrw = jnp.stack([_pad_rows(rwkv_w_up[l], 0, RWKV_LR), _pad_rows(rwkv_a_up[l], 32, RWKV_LR),
                         _pad_rows(rwkv_g_up[l], 64, RWKV_LR)])
        pv = _rows8(rwkv_w0[l], rwkv_a0[l], rwkv_k_k[l], rwkv_k_a[l], rwkv_r_k[l])
        *scan_ops, g, bv = _rwkv_prep(p_rwkv.reshape(nb, seq, RWKV_COLS), tshift_mu[l].reshape(1, -1), pv, lrw,
                                      ones_blk, nb, seq)
        y = _rwkv_scan(*scan_ops, nb)
        y_a = _rwkv_post(y, g, bv, _rows8(rwkv_gn_g[l], rwkv_gn_b[l]), ones_blk, nb, seq)
        y_a = y_a.reshape(nb * seq, GROUP)

        y_b = _lru(p_lru, lru_conv_w[l], _rows8(lru_conv_b[l], lru_b_r[l], lru_b_i[l], lru_lambda[l]),
                   _block_diag(lru_w_r[l]).astype(BF16), _block_diag(lru_w_i[l]).astype(BF16), nb, seq)

        y_c = _moba(*_rope_moba(p_att, cos, sin, seq), nb, seq)

        y_d = _retention(p_ret, cos, sin, _rows8(ret_gn_g[l], ret_gn_b[l]), nb, seq)

        x2d = _out_proj((y_a, y_b, y_c, y_d), x2d, w_out[l].astype(BF16), _rows8(ln1_g[l], ln1_b[l]), alpha)
        act = _ffn_up(x2d, ffn_w_up[l].astype(BF16), ffn_conv_w[l], ffn_conv_b[l].reshape(1, -1), seq)
        x2d = _ffn_down(act, x2d, ffn_w_down[l].astype(BF16), _rows8(ln2_g[l], ln2_b[l]), alpha)
    return x2d.reshape(nb, seq, D_MODEL)
```

```python
import functools
import math

import jax
import jax.numpy as jnp
from jax import lax
from jax.experimental import pallas as pl
from jax.experimental.pallas import tpu as pltpu

F32 = jnp.float32
BF16 = jnp.bfloat16
HIGHEST = lax.Precision.HIGHEST

D_MODEL = 1024
HEAD_DIM = 64
GROUP = 256
HEADS = GROUP // HEAD_DIM
RWKV_COLS = 3 * GROUP + 32 + 32 + 64
RWKV_LR = RWKV_COLS - 3 * GROUP
RWKV_GN_EPS = 64e-5
LRU_C = 8.0
MOBA_BLOCK = 256
MOBA_TOPK = 3
ROPE_THETA = 10000.0
NEG_INF = -1e30
GN_EPS = 1e-5
D_FF = 2816
LN_EPS = 1e-5
RET_CHUNK = 256

VMEM_LIMIT_BYTES = 56 * 1024 * 1024
SUBLANES = 8
LANES = 128


def _cparams(*semantics):
    return pltpu.CompilerParams(dimension_semantics=semantics, vmem_limit_bytes=VMEM_LIMIT_BYTES)


def _hdot(a, b):
    return jnp.dot(a, b, precision=HIGHEST, preferred_element_type=F32)


def _bdot(a, b):
    return jnp.dot(a.astype(BF16), b.astype(BF16), preferred_element_type=F32)


def _bdot_t(a, b):
    return lax.dot_general(a.astype(BF16), b.astype(BF16), (((1,), (1,)), ((), ())),
                           preferred_element_type=F32)


def _softplus(z):
    return jnp.maximum(z, 0.0) + jnp.log1p(jnp.exp(-jnp.abs(z)))


def _gelu_tanh(x):
    return 0.5 * x * (1.0 + jnp.tanh(math.sqrt(2.0 / math.pi) * (x + 0.044715 * (x * x * x))))


def _shift_rows(x, halo, s):
    rolled = pltpu.roll(x, s, axis=0)
    row8 = lax.broadcasted_iota(jnp.int32, halo.shape, 0)
    top = jnp.where(row8 < s, pltpu.roll(halo, s, axis=0), rolled[0:SUBLANES])
    return jnp.concatenate([top, rolled[SUBLANES:]], axis=0)


def _layer_norm(y, g, b):
    mu = jnp.mean(y, axis=-1, keepdims=True)
    d = y - mu
    var = jnp.mean(d * d, axis=-1, keepdims=True)
    return d * lax.rsqrt(var + LN_EPS) * g + b


def _halo_map(tm, rows=SUBLANES):
    return lambda i: (jnp.maximum(i * (tm // rows) - 1, 0), 0)


def _split3(x):
    hi = x.astype(BF16)
    r = x - hi.astype(F32)
    mid = r.astype(BF16)
    lo = (r - mid.astype(F32)).astype(BF16)
    return hi, mid, lo


def _seg_sum(x, ones_blk):
    return sum(jnp.dot(part, ones_blk, preferred_element_type=F32) for part in _split3(x))


def _dot3(a, b):
    a_hi, a_lo, _ = _split3(a)
    b_hi, b_lo, _ = _split3(b)
    dot = functools.partial(jnp.dot, preferred_element_type=F32)
    return dot(a_hi, b_hi) + (dot(a_hi, b_lo) + dot(a_lo, b_hi))


IN_WIDTHS = (RWKV_COLS, 2 * GROUP, 3 * GROUP, 4 * GROUP)
IN_COLS = sum(IN_WIDTHS)


def _in_proj_kernel(x_ref, w_ref, *outs):
    xb = x_ref[...].astype(BF16)
    col = 0
    for o in outs:
        n = o.shape[1]
        o[...] = jnp.dot(xb, w_ref[:, col:col + n], preferred_element_type=F32)
        col += n


def _in_proj(x2d, w_bf16, tm=512):
    m = x2d.shape[0]
    return pl.pallas_call(
        _in_proj_kernel,
        grid=(m // tm,),
        in_specs=[pl.BlockSpec((tm, D_MODEL), lambda i: (i, 0)),
                  pl.BlockSpec((D_MODEL, IN_COLS), lambda i: (0, 0))],
        out_specs=[pl.BlockSpec((tm, w), lambda i: (i, 0)) for w in IN_WIDTHS],
        out_shape=[jax.ShapeDtypeStruct((m, w), F32) for w in IN_WIDTHS],
        compiler_params=_cparams("arbitrary"),
        name="in_proj",
    )(x2d, w_bf16)


def _rows_to_lanes(x_s, out_ref, t0, nb, tb, gw, dup):
    steps = LANES // gw
    lane_grp = lax.broadcasted_iota(jnp.int32, (nb, LANES), 1) // gw
    z = [None] * (GROUP // gw)
    for tt in range(steps):
        xt = [x_s[c, pl.ds(t0 + tt, nb, stride=tb), :] for c in range(GROUP // LANES)]
        for g in range(len(z)):
            half = xt[g // steps]
            shift = (tt - g % steps) % steps * gw
            piece = pltpu.roll(half, shift, axis=1) if shift else half
            z[g] = piece if tt == 0 else jnp.where(lane_grp == tt, piece, z[g])
    tile = jnp.concatenate([zg for zg in z for _ in range(dup)], axis=0)
    out_ref[pl.ds(t0, steps)] = tile.T.reshape(steps, gw, tile.shape[0])


def _lanes_to_rows(y_ref, y_s, t0, nb, tb, gw):
    steps = LANES // gw
    tile = y_ref[pl.ds(t0, steps)]
    tile = tile.reshape(steps * gw, tile.shape[2]).T
    lane_grp = lax.broadcasted_iota(jnp.int32, (nb, LANES), 1) // gw
    for tt in range(steps):
        for half in range(GROUP // LANES):
            cols = None
            for pos in range(steps):
                g = half * steps + pos
                piece = tile[g * nb:(g + 1) * nb, :]
                shift = (pos - tt) % steps * gw
                piece = pltpu.roll(piece, shift, axis=1) if shift else piece
                cols = piece if pos == 0 else jnp.where(lane_grp == pos, piece, cols)
            y_s[half, pl.ds(t0 + tt, nb, stride=tb), :] = cols


def _rwkv_prep_kernel(p_ref, halo_ref, mu_ref, pv_ref, lrw_ref, ones_ref,
                      r_o, k_o, v_o, kk_o, b_o, w_o, g_o, bv_o, *stage, nb, tb):
    rows = nb * tb
    pa = p_ref[...].reshape(rows, RWKV_COLS)
    prev = jnp.where(pl.program_id(0) == 0, 0.0, halo_ref[:, SUBLANES - 1:SUBLANES, :])
    prev = jnp.broadcast_to(prev, (nb, tb, RWKV_COLS)).reshape(rows, RWKV_COLS)
    t_idx = lax.broadcasted_iota(jnp.int32, pa.shape, 0) % tb
    shifted = jnp.where(t_idx == 0, prev, pltpu.roll(pa, 1, axis=0))
    pa = pa + (shifted - pa) * mu_ref[...]
    r = pa[:, 0:GROUP]
    k = pa[:, GROUP:2 * GROUP]
    v = pa[:, 2 * GROUP:3 * GROUP]
    lr = pa[:, 3 * GROUP:RWKV_COLS]
    w0, a0, k_k, k_a, r_k = (pv_ref[n:n + 1, :] for n in range(5))
    tw = _dot3(jnp.tanh(lr), lrw_ref[0])
    ta = _dot3(lr, lrw_ref[1])
    g = _dot3(jax.nn.sigmoid(lr), lrw_ref[2])
    w_log = -_softplus(-(w0 + tw)) - 0.5
    decay = jnp.exp(-jnp.exp(w_log))
    a = jax.nn.sigmoid(a0 + ta)
    kk = k * k_k
    k2 = k * (1.0 + (a - 1.0) * k_a)
    ss = _seg_sum(kk * kk, ones_ref[...])
    kkn = kk / jnp.maximum(jnp.sqrt(ss), 1e-12)
    bonus = _seg_sum(r * k2 * r_k, ones_ref[...])
    g_o[...] = g.reshape(nb, tb, GROUP)
    bv_o[...] = (bonus * v).reshape(nb, tb, GROUP)
    layouts = ((r, r_o, HEAD_DIM, 2), (k2, k_o, HEAD_DIM, 2), (v, v_o, HEAD_DIM // 2, 1),
               (kkn, kk_o, HEAD_DIM, 2), (kkn * a, b_o, HEAD_DIM, 2), (decay, w_o, HEAD_DIM, 2))
    for (x, _, _, _), x_s in zip(layouts, stage):
        for c in range(GROUP // LANES):
            x_s[c] = x[:, c * LANES:(c + 1) * LANES]
    span = LANES // (HEAD_DIM // 2)

    def move(n, carry):
        for (_, out, gw, dup), x_s in zip(layouts, stage):
            for t0 in range(0, span, LANES // gw):
                _rows_to_lanes(x_s, out, n * span + t0, nb, tb, gw, dup)
        return carry

    lax.fori_loop(0, tb // span, move, 0)


RWKV_TB = 32


def _rwkv_prep(p_rwkv, mu, pv, lrw, ones_blk, nb, seq, tb=RWKV_TB):
    lanes = 2 * HEADS * nb
    full = lambda a: pl.BlockSpec(a.shape, lambda i: (0,) * a.ndim)
    key_spec = pl.BlockSpec((tb, HEAD_DIM, lanes), lambda i: (i, 0, 0))
    val_spec = pl.BlockSpec((tb, HEAD_DIM // 2, lanes), lambda i: (i, 0, 0))
    nat_spec = pl.BlockSpec((nb, tb, GROUP), lambda i: (0, i, 0))
    key_shape = jax.ShapeDtypeStruct((seq, HEAD_DIM, lanes), F32)
    val_shape = jax.ShapeDtypeStruct((seq, HEAD_DIM // 2, lanes), F32)
    nat_shape = jax.ShapeDtypeStruct((nb, seq, GROUP), F32)
    return pl.pallas_call(
        functools.partial(_rwkv_prep_kernel, nb=nb, tb=tb),
        grid=(seq // tb,),
        in_specs=[pl.BlockSpec((nb, tb, RWKV_COLS), lambda i: (0, i, 0)),
                  pl.BlockSpec((nb, SUBLANES, RWKV_COLS),
                               lambda i: (0, jnp.maximum(i * (tb // SUBLANES) - 1, 0), 0)),
                  full(mu), full(pv), full(lrw), full(ones_blk)],
        out_specs=[key_spec, key_spec, val_spec, key_spec, key_spec, key_spec, nat_spec, nat_spec],
        out_shape=[key_shape, key_shape, val_shape, key_shape, key_shape, key_shape, nat_shape, nat_shape],
        scratch_shapes=[pltpu.VMEM((GROUP // LANES, nb * tb, LANES), F32)] * 6,
        compiler_params=_cparams("arbitrary"),
        name="rwkv_prep",
    )(p_rwkv, p_rwkv, mu, pv, lrw, ones_blk)


def _rwkv_scan_kernel(r_ref, k_ref, v_ref, kk_ref, b_ref, w_ref, y_ref, s_ref, *, tb):
    n_j = s_ref.shape[0]
    tile = s_ref.shape[1:]

    @pl.when(pl.program_id(0) == 0)
    def _():
        s_ref[...] = jnp.zeros_like(s_ref)

    sa0 = jnp.zeros(tile, F32)
    for j in range(n_j):
        sa0 = sa0 + s_ref[j] * kk_ref[0, pl.ds(j, 1), :]

    def step(t, sa):
        t_next = jnp.minimum(t + 1, tb - 1)
        vt = v_ref[t]
        yacc = jnp.zeros(tile, F32)
        nacc = jnp.zeros(tile, F32)
        for j in range(n_j):
            sj = (s_ref[j] * w_ref[t, pl.ds(j, 1), :] - sa * b_ref[t, pl.ds(j, 1), :]
                  + vt * k_ref[t, pl.ds(j, 1), :])
            s_ref[j] = sj
            yacc = yacc + sj * r_ref[t, pl.ds(j, 1), :]
            nacc = nacc + sj * kk_ref[t_next, pl.ds(j, 1), :]
        y_ref[t] = yacc
        return nacc

    lax.fori_loop(0, tb, step, sa0)


def _rwkv_scan(r, k, v, kk, b, w, tb=64):
    seq, n_j, lanes = r.shape
    kspec = pl.BlockSpec((tb, n_j, lanes), lambda i: (i, 0, 0))
    vspec = pl.BlockSpec((tb,) + v.shape[1:], lambda i: (i, 0, 0))
    return pl.pallas_call(
        functools.partial(_rwkv_scan_kernel, tb=tb),
        grid=(seq // tb,),
        in_specs=[kspec, kspec, vspec, kspec, kspec, kspec],
        out_specs=vspec,
        out_shape=jax.ShapeDtypeStruct(v.shape, F32),
        scratch_shapes=[pltpu.VMEM((n_j,) + v.shape[1:], F32)],
        compiler_params=_cparams("arbitrary"),
        name="rwkv_scan",
    )(r, k, v, kk, b, w)


def _rwkv_post_kernel(y_ref, g_ref, bv_ref, gn_ref, ones_ref, o_ref, y_s, *, nb, tb):
    gw = y_ref.shape[1]
    steps = LANES // gw
    unroll = 2

    def move(n, carry):
        for u in range(unroll):
            _lanes_to_rows(y_ref, y_s, (n * unroll + u) * steps, nb, tb, gw)
        return carry

    lax.fori_loop(0, tb // (steps * unroll), move, 0)
    y = jnp.concatenate([y_s[c] for c in range(GROUP // LANES)], axis=1)
    inv_n = 1.0 / HEAD_DIM
    mu = _seg_sum(y, ones_ref[...]) * inv_n
    d = y - mu
    var = _seg_sum(d * d, ones_ref[...]) * inv_n
    yn = d * lax.rsqrt(var + RWKV_GN_EPS) * gn_ref[0:1, :] + gn_ref[1:2, :]
    out = (yn + bv_ref[...].reshape(nb * tb, GROUP)) * g_ref[...].reshape(nb * tb, GROUP)
    o_ref[...] = out.reshape(nb, tb, GROUP)


def _rwkv_post(y, g, bv, gn, ones_blk, nb, seq, tb=RWKV_TB):
    nat_spec = pl.BlockSpec((nb, tb, GROUP), lambda i: (0, i, 0))
    full = lambda a: pl.BlockSpec(a.shape, lambda i: (0,) * a.ndim)
    return pl.pallas_call(
        functools.partial(_rwkv_post_kernel, nb=nb, tb=tb),
        grid=(seq // tb,),
        in_specs=[pl.BlockSpec((tb,) + y.shape[1:], lambda i: (i, 0, 0)), nat_spec, nat_spec,
                  full(gn), full(ones_blk)],
        out_specs=nat_spec,
        out_shape=jax.ShapeDtypeStruct((nb, seq, GROUP), F32),
        scratch_shapes=[pltpu.VMEM((GROUP // LANES, nb * tb, LANES), F32)],
        compiler_params=_cparams("arbitrary"),
        name="rwkv_post",
    )(y, g, bv, gn, ones_blk)


def _lru_kernel(p_ref, halo_ref, cw_ref, pv_ref, wr_ref, wi_ref, o_ref, hc_ref, *, tt):
    j = pl.program_id(1)
    x = p_ref[:, 0:GROUP]
    gb = p_ref[:, GROUP:2 * GROUP]
    halo = jnp.where(j == 0, 0.0, halo_ref[:, 0:GROUP])
    conv_b, b_r, b_i, lam = (pv_ref[n:n + 1, :] for n in range(4))
    xc = x * cw_ref[3:4, :] + conv_b
    for s in (1, 2, 3):
        xc = xc + _shift_rows(x, halo, s) * cw_ref[3 - s:4 - s, :]
    gate_r = jax.nn.sigmoid(_bdot(xc, wr_ref[...]) + b_r)
    gate_i = jax.nn.sigmoid(_bdot(xc, wi_ref[...]) + b_i)
    log_a = -LRU_C * gate_r * _softplus(-lam)
    z = 2.0 * log_a
    expm1_z = jnp.tanh(0.5 * z) * (jnp.exp(z) + 1.0)
    a = jnp.exp(log_a)
    u = jnp.sqrt(-expm1_z) * (gate_i * xc)

    @pl.when(j == 0)
    def _():
        hc_ref[...] = jnp.zeros_like(hc_ref)

    row = lax.broadcasted_iota(jnp.int32, a.shape, 0)
    d = 1
    while d < tt:
        keep = row >= d
        u = a * jnp.where(keep, pltpu.roll(u, d, axis=0), 0.0) + u
        a = a * jnp.where(keep, pltpu.roll(a, d, axis=0), 1.0)
        d *= 2
    h = a * hc_ref[...] + u
    hc_ref[...] = h[tt - 1:tt, :]
    o_ref[...] = _gelu_tanh(gb) * h


def _lru(p_lru, cw, pv, wr, wi, nb, seq, tt=256):
    m = p_lru.shape[0]
    nt = seq // tt
    full = lambda a: pl.BlockSpec(a.shape, lambda b, j: (0,) * a.ndim)
    halo = _halo_map(tt)
    return pl.pallas_call(
        functools.partial(_lru_kernel, tt=tt),
        grid=(nb, nt),
        in_specs=[pl.BlockSpec((tt, 2 * GROUP), lambda b, j: (b * nt + j, 0)),
                  pl.BlockSpec((SUBLANES, 2 * GROUP), lambda b, j: halo(b * nt + j)),
                  full(cw), full(pv), full(wr), full(wi)],
        out_specs=pl.BlockSpec((tt, GROUP), lambda b, j: (b * nt + j, 0)),
        out_shape=jax.ShapeDtypeStruct((m, GROUP), F32),
        scratch_shapes=[pltpu.VMEM((1, GROUP), F32)],
        compiler_params=_cparams("arbitrary", "arbitrary"),
        name="lru",
    )(p_lru, p_lru, cw, pv, wr, wi)


def _rope_rows(x, cos, sin_signed):
    lane = lax.broadcasted_iota(jnp.int32, x.shape, 1)
    first_half = lane % HEAD_DIM < HEAD_DIM // 2
    width = x.shape[1]
    partner = jnp.where(first_half, pltpu.roll(x, width - HEAD_DIM // 2, axis=1),
                        pltpu.roll(x, HEAD_DIM // 2, axis=1))
    return x * cos + partner * sin_signed


def _rope_moba_kernel(p_ref, cos_ref, sin_ref, qt_o, k_o, vt_o, km_o):
    cos = cos_ref[...]
    sin = sin_ref[...]
    q = _rope_rows(p_ref[:, 0:GROUP], cos, sin) * HEAD_DIM ** -0.5
    qt_o[...] = q.T
    k = _rope_rows(p_ref[:, GROUP:2 * GROUP], cos, sin)
    for h in range(HEADS):
        k_o[h] = k[:, h * HEAD_DIM:(h + 1) * HEAD_DIM].astype(BF16)
    km_o[0] = jnp.mean(k, axis=0, keepdims=True)
    vt_o[...] = p_ref[:, 2 * GROUP:3 * GROUP].T.astype(BF16)


def _rope_moba(p, cos, sin, seq):
    m = p.shape[0]
    tm = MOBA_BLOCK
    nt = seq // tm
    tab = pl.BlockSpec((tm, GROUP), lambda i: (i % nt, 0))
    col = pl.BlockSpec((GROUP, tm), lambda i: (0, i))
    return pl.pallas_call(
        _rope_moba_kernel,
        grid=(m // tm,),
        in_specs=[pl.BlockSpec((tm, 3 * GROUP), lambda i: (i, 0)), tab, tab],
        out_specs=[col, pl.BlockSpec((HEADS, tm, HEAD_DIM), lambda i: (0, i, 0)), col,
                   pl.BlockSpec((1, 1, GROUP), lambda i: (i, 0, 0))],
        out_shape=[jax.ShapeDtypeStruct((GROUP, m), F32), jax.ShapeDtypeStruct((HEADS, m, HEAD_DIM), BF16),
                   jax.ShapeDtypeStruct((GROUP, m), BF16), jax.ShapeDtypeStruct((m // tm, 1, GROUP), F32)],
        compiler_params=_cparams("arbitrary"),
        name="rope_moba",
    )(p, cos, sin)


def _rope_tables(seq):
    inv = ROPE_THETA ** (-jnp.arange(0, HEAD_DIM, 2, dtype=F32) / HEAD_DIM)
    ang = jnp.arange(seq, dtype=F32)[:, None] * inv[None, :]
    cos, sin = jnp.cos(ang), jnp.sin(ang)
    cos_full = jnp.tile(jnp.concatenate([cos, cos], axis=1), (1, HEADS))
    sin_signed = jnp.tile(jnp.concatenate([-sin, sin], axis=1), (1, HEADS))
    return cos_full, sin_signed


def _moba_kernel(qt_ref, k_ref, vt_ref, km_ref, o_ref, m_s, l_s, acc_s, s_s, p_s):
    blk = pl.program_id(1)
    tq = qt_ref.shape[1]
    n_blocks = km_ref.shape[0]
    key_i = lax.broadcasted_iota(jnp.int32, (MOBA_BLOCK, tq), 0)
    qry_i = lax.broadcasted_iota(jnp.int32, (MOBA_BLOCK, tq), 1)
    blk_id = lax.broadcasted_iota(jnp.int32, (n_blocks, tq), 0)
    valid = blk_id < blk
    heads = [slice(h * HEAD_DIM, (h + 1) * HEAD_DIM) for h in range(HEADS)]

    sels = []
    for hs in heads:
        gate = _hdot(km_ref[:, hs], qt_ref[hs, :])
        sel = jnp.zeros((n_blocks, tq), F32)
        for n in range(n_blocks):
            gn = gate[n:n + 1, :]
            beats = valid & ((gate > gn) | ((gate == gn) & (blk_id < n)))
            cnt = jnp.sum(beats.astype(F32), axis=0, keepdims=True)
            sel = jnp.where((blk_id == n) & (cnt < MOBA_TOPK) & valid, 1.0, sel)
        sels.append(sel)

    def attend(start, keep_fn, first):
        for h, hs in enumerate(heads):
            s_s[h] = jnp.dot(k_ref[h, pl.ds(start, MOBA_BLOCK), :], qt_ref[hs, :].astype(BF16),
                             preferred_element_type=F32)
        alphas = []
        for h in range(HEADS):
            s = jnp.where(keep_fn(h), s_s[h], NEG_INF)
            m_blk = jnp.max(s, axis=0, keepdims=True)
            if first:
                m_new = m_blk
            else:
                m_new = jnp.maximum(m_s[h], m_blk)
                alphas.append(jnp.exp(m_s[h] - m_new))
            p = jnp.exp(s - m_new)
            p_s[h] = p.astype(BF16)
            p_sum = jnp.sum(p, axis=0, keepdims=True)
            l_s[h] = p_sum if first else alphas[h] * l_s[h] + p_sum
            m_s[h] = m_new
        for h, hs in enumerate(heads):
            pv = jnp.dot(vt_ref[hs, pl.ds(start, MOBA_BLOCK)], p_s[h], preferred_element_type=F32)
            acc_s[hs, :] = pv if first else alphas[h] * acc_s[hs, :] + pv

    causal = key_i <= qry_i
    attend(pl.multiple_of(blk * MOBA_BLOCK, MOBA_BLOCK), lambda h: causal, True)

    def body(n, carry):
        picked = lambda h: jnp.sum(jnp.where(blk_id == n, sels[h], 0.0), axis=0, keepdims=True) > 0.5
        attend(pl.multiple_of(n * MOBA_BLOCK, MOBA_BLOCK), picked, False)
        return carry

    lax.fori_loop(0, blk, body, 0)
    out_t = jnp.concatenate([acc_s[hs, :] / l_s[h] for h, hs in enumerate(heads)], axis=0)
    o_ref[...] = out_t.T


def _moba(qt, kh, vt, kmean, nb, seq):
    m = qt.shape[1]
    nblk = seq // MOBA_BLOCK
    return pl.pallas_call(
        _moba_kernel,
        grid=(nb, nblk),
        in_specs=[pl.BlockSpec((GROUP, MOBA_BLOCK), lambda b, j: (0, b * nblk + j)),
                  pl.BlockSpec((HEADS, seq, HEAD_DIM), lambda b, j: (0, b, 0)),
                  pl.BlockSpec((GROUP, seq), lambda b, j: (0, b)),
                  pl.BlockSpec((None, nblk, GROUP), lambda b, j: (b, 0, 0))],
        out_specs=pl.BlockSpec((MOBA_BLOCK, GROUP), lambda b, j: (b * nblk + j, 0)),
        out_shape=jax.ShapeDtypeStruct((m, GROUP), F32),
        scratch_shapes=[pltpu.VMEM((HEADS, 1, MOBA_BLOCK), F32), pltpu.VMEM((HEADS, 1, MOBA_BLOCK), F32),
                        pltpu.VMEM((GROUP, MOBA_BLOCK), F32),
                        pltpu.VMEM((HEADS, MOBA_BLOCK, MOBA_BLOCK), F32),
                        pltpu.VMEM((HEADS, MOBA_BLOCK, MOBA_BLOCK), BF16)],
        compiler_params=_cparams("arbitrary", "arbitrary"),
        name="moba",
    )(qt, kh, vt, kmean.reshape(nb, nblk, GROUP))


def _ret_kernel(p_ref, cos_ref, sin_ref, gn_ref, o_ref, st_ref, mask_s):
    lc = p_ref.shape[0]
    log_gammas = [math.log1p(-(2.0 ** (-5.0 - h))) for h in range(HEADS)]

    @pl.when(pl.program_id(1) == 0)
    def _():
        st_ref[...] = jnp.zeros_like(st_ref)

    @pl.when((pl.program_id(0) == 0) & (pl.program_id(1) == 0))
    def _():
        ri = lax.broadcasted_iota(jnp.int32, (lc, lc), 0)
        ci = lax.broadcasted_iota(jnp.int32, (lc, lc), 1)
        diff = (ri - ci).astype(F32)
        for h in range(HEADS):
            mask_s[h] = jnp.where(diff >= 0, jnp.exp(log_gammas[h] * jnp.maximum(diff, 0.0)), 0.0)

    cos = cos_ref[...]
    sin = sin_ref[...]
    q_all = _rope_rows(p_ref[:, 0:GROUP], cos, sin)
    k_all = _rope_rows(p_ref[:, GROUP:2 * GROUP], cos, sin) * HEAD_DIM ** -0.5
    pos = lax.broadcasted_iota(jnp.int32, (lc, 1), 0).astype(F32)
    for h in range(HEADS):
        hs = slice(h * HEAD_DIM, (h + 1) * HEAD_DIM)
        log_gamma = log_gammas[h]
        q = q_all[:, hs]
        k = k_all[:, hs]
        v = p_ref[:, 2 * GROUP + h * HEAD_DIM:2 * GROUP + (h + 1) * HEAD_DIM]
        g = p_ref[:, 3 * GROUP + h * HEAD_DIM:3 * GROUP + (h + 1) * HEAD_DIM]
        intra = _bdot_t(q, k) * mask_s[h]
        state = st_ref[h]
        out = _bdot(intra, v) + _bdot(q * jnp.exp(log_gamma * (pos + 1.0)), state)
        k_w = jnp.exp(log_gamma * (lc - 1.0 - pos))
        kv = lax.dot_general((k * k_w).astype(BF16), v.astype(BF16), (((0,), (0,)), ((), ())),
                             preferred_element_type=F32)
        st_ref[h] = state * math.exp(log_gamma * lc) + kv
        mu = jnp.mean(out, axis=-1, keepdims=True)
        d = out - mu
        var = jnp.mean(d * d, axis=-1, keepdims=True)
        yn = d * lax.rsqrt(var + GN_EPS) * gn_ref[0:1, hs] + gn_ref[1:2, hs]
        o_ref[:, hs] = g * jax.nn.sigmoid(g) * yn


def _retention(p_ret, cos, sin, gn, nb, seq):
    m = p_ret.shape[0]
    lc = RET_CHUNK
    nt = seq // lc
    tab = pl.BlockSpec((lc, GROUP), lambda b, j: (j, 0))
    return pl.pallas_call(
        _ret_kernel,
        grid=(nb, nt),
        in_specs=[pl.BlockSpec((lc, 4 * GROUP), lambda b, j: (b * nt + j, 0)), tab, tab,
                  pl.BlockSpec(gn.shape, lambda b, j: (0, 0))],
        out_specs=pl.BlockSpec((lc, GROUP), lambda b, j: (b * nt + j, 0)),
        out_shape=jax.ShapeDtypeStruct((m, GROUP), F32),
        scratch_shapes=[pltpu.VMEM((HEADS, HEAD_DIM, HEAD_DIM), F32), pltpu.VMEM((HEADS, lc, lc), F32)],
        compiler_params=_cparams("arbitrary", "arbitrary"),
        name="retention",
    )(p_ret, cos, sin, gn)


def _out_proj_kernel(ya_ref, yb_ref, yc_ref, yd_ref, x_ref, w_ref, ln_ref, o_ref, *, alpha):
    y = jnp.concatenate([r[...].astype(BF16) for r in (ya_ref, yb_ref, yc_ref, yd_ref)], axis=1)
    acc = alpha * x_ref[...] + jnp.dot(y, w_ref[...], preferred_element_type=F32)
    o_ref[...] = _layer_norm(acc, ln_ref[0:1, :], ln_ref[1:2, :])


def _out_proj(ys, x2d, w_bf16, ln, alpha, tm=512):
    m = x2d.shape[0]
    yspec = pl.BlockSpec((tm, GROUP), lambda i: (i, 0))
    xspec = pl.BlockSpec((tm, D_MODEL), lambda i: (i, 0))
    full = lambda a: pl.BlockSpec(a.shape, lambda i: (0,) * a.ndim)
    return pl.pallas_call(
        functools.partial(_out_proj_kernel, alpha=alpha),
        grid=(m // tm,),
        in_specs=[yspec] * 4 + [xspec, full(w_bf16), full(ln)],
        out_specs=xspec,
        out_shape=jax.ShapeDtypeStruct((m, D_MODEL), F32),
        compiler_params=_cparams("arbitrary"),
        name="out_proj_ln",
    )(*ys, x2d, w_bf16, ln)


FFN_TN = 256


def _ffn_up_kernel(x_ref, w_ref, cw_ref, cb_ref, o_ref, tail_ref, h_s, *, tiles_per_seq):
    xb = x_ref[...].astype(BF16)
    tm = xb.shape[0]
    first = pl.program_id(0) % tiles_per_seq == 0

    @pl.when(pl.program_id(0) == 0)
    def _():
        tail_ref[...] = jnp.zeros_like(tail_ref)

    def conv_cols(c0, slot):
        h = jnp.dot(xb, w_ref[:, c0:c0 + FFN_TN], preferred_element_type=F32)
        h_s[slot, 0:SUBLANES, :] = jnp.where(first, 0.0, tail_ref[:, c0:c0 + FFN_TN])
        h_s[slot, SUBLANES:, :] = h
        tail_ref[:, c0:c0 + FFN_TN] = h[tm - SUBLANES:tm]
        u = h * cw_ref[2:3, c0:c0 + FFN_TN] + cb_ref[:, c0:c0 + FFN_TN]
        for s in (1, 2):
            u = u + h_s[slot, pl.ds(SUBLANES - s, tm), :] * cw_ref[2 - s:3 - s, c0:c0 + FFN_TN]
        return u

    for c in range(D_FF // FFN_TN):
        gate = conv_cols(c * FFN_TN, 0)
        val = conv_cols(D_FF + c * FFN_TN, 1)
        o_ref[:, c * FFN_TN:(c + 1) * FFN_TN] = (_gelu_tanh(gate) * val).astype(BF16)


def _ffn_up(x2d, w_bf16, cw, cb, seq, tm=512):
    m = x2d.shape[0]
    full = lambda a: pl.BlockSpec(a.shape, lambda i: (0,) * a.ndim)
    return pl.pallas_call(
        functools.partial(_ffn_up_kernel, tiles_per_seq=seq // tm),
        grid=(m // tm,),
        in_specs=[pl.BlockSpec((tm, D_MODEL), lambda i: (i, 0)), full(w_bf16), full(cw), full(cb)],
        out_specs=pl.BlockSpec((tm, D_FF), lambda i: (i, 0)),
        out_shape=jax.ShapeDtypeStruct((m, D_FF), BF16),
        scratch_shapes=[pltpu.VMEM((SUBLANES, 2 * D_FF), F32), pltpu.VMEM((2, SUBLANES + tm, FFN_TN), F32)],
        compiler_params=_cparams("arbitrary"),
        name="ffn_up",
    )(x2d, w_bf16, cw, cb)


def _ffn_down_kernel(a_ref, x_ref, w_ref, ln_ref, o_ref, *, alpha):
    acc = alpha * x_ref[...] + jnp.dot(a_ref[...], w_ref[...], preferred_element_type=F32)
    o_ref[...] = _layer_norm(acc, ln_ref[0:1, :], ln_ref[1:2, :])


def _ffn_down(act, x2d, w_bf16, ln, alpha, tm=512):
    m = x2d.shape[0]
    xspec = pl.BlockSpec((tm, D_MODEL), lambda i: (i, 0))
    full = lambda a: pl.BlockSpec(a.shape, lambda i: (0,) * a.ndim)
    return pl.pallas_call(
        functools.partial(_ffn_down_kernel, alpha=alpha),
        grid=(m // tm,),
        in_specs=[pl.BlockSpec((tm, D_FF), lambda i: (i, 0)), xspec, full(w_bf16), full(ln)],
        out_specs=xspec,
        out_shape=jax.ShapeDtypeStruct((m, D_MODEL), F32),
        compiler_params=_cparams("arbitrary"),
        name="ffn_down_ln",
    )(act, x2d, w_bf16, ln)


def _block_diag(w):
    h, n, _ = w.shape
    eye = jnp.eye(h, dtype=w.dtype)
    return (eye[:, None, :, None] * w[:, :, None, :]).reshape(h * n, h * n)


def _pad_rows(w, offset, total):
    return jnp.zeros((total, w.shape[1]), w.dtype).at[offset:offset + w.shape[0]].set(w)


def _rows8(*vecs):
    rows = [v.reshape(1, -1) for v in vecs]
    rows += [jnp.zeros_like(rows[0])] * (SUBLANES - len(rows))
    return jnp.concatenate(rows, axis=0)


def kernel(x, w_in, tshift_mu, rwkv_w0, rwkv_w_up, rwkv_a0, rwkv_a_up, rwkv_g_up, rwkv_k_k, rwkv_k_a, rwkv_r_k, rwkv_gn_g, rwkv_gn_b, lru_conv_w, lru_conv_b, lru_w_r, lru_b_r, lru_w_i, lru_b_i, lru_lambda, ret_gn_g, ret_gn_b, w_out, ln1_g, ln1_b, ffn_w_up, ffn_conv_w, ffn_conv_b, ffn_w_down, ln2_g, ln2_b):
    nb, seq, _ = x.shape
    depth = w_in.shape[0]
    alpha = (2.0 * depth) ** 0.25
    cos, sin = _rope_tables(seq)
    ones_blk = _block_diag(jnp.ones((HEADS, HEAD_DIM, HEAD_DIM), F32)).astype(BF16)
    x2d = x.reshape(nb * seq, D_MODEL)
    for l in range(depth):
        p_rwkv, p_lru, p_att, p_ret = _in_proj(x2d, w_in[l].astype(BF16))

        lrw = jnp.stack([_pad_rows(rwkv_w_up[l], 0, RWKV_LR), _pad_rows(rwkv_a_up[l], 32, RWKV_LR),
                         _pad_rows(rwkv_g_up[l], 64, RWKV_LR)])
        pv = _rows8(rwkv_w0[l], rwkv_a0[l], rwkv_k_k[l], rwkv_k_a[l], rwkv_r_k[l])
        *scan_ops, g, bv = _rwkv_prep(p_rwkv.reshape(nb, seq, RWKV_COLS), tshift_mu[l].reshape(1, -1), pv, lrw,
                                      ones_blk, nb, seq)
        y = _rwkv_scan(*scan_ops)
        y_a = _rwkv_post(y, g, bv, _rows8(rwkv_gn_g[l], rwkv_gn_b[l]), ones_blk, nb, seq)
        y_a = y_a.reshape(nb * seq, GROUP)

        y_b = _lru(p_lru, lru_conv_w[l], _rows8(lru_conv_b[l], lru_b_r[l], lru_b_i[l], lru_lambda[l]),
                   _block_diag(lru_w_r[l]).astype(BF16), _block_diag(lru_w_i[l]).astype(BF16), nb, seq)

        y_c = _moba(*_rope_moba(p_att, cos, sin, seq), nb, seq)

        y_d = _retention(p_ret, cos, sin, _rows8(ret_gn_g[l], ret_gn_b[l]), nb, seq)

        x2d = _out_proj((y_a, y_b, y_c, y_d), x2d, w_out[l].astype(BF16), _rows8(ln1_g[l], ln1_b[l]), alpha)
        act = _ffn_up(x2d, ffn_w_up[l].astype(BF16), ffn_conv_w[l], ffn_conv_b[l].reshape(1, -1), seq)
        x2d = _ffn_down(act, x2d, ffn_w_down[l].astype(BF16), _rows8(ln2_g[l], ln2_b[l]), alpha)
    return x2d.reshape(nb, seq, D_MODEL)
```

```python
import functools
import math

import jax
import jax.numpy as jnp
from jax import lax
from jax.experimental import pallas as pl
from jax.experimental.pallas import tpu as pltpu

F32 = jnp.float32
BF16 = jnp.bfloat16
HIGHEST = lax.Precision.HIGHEST

D_MODEL = 1024
HEAD_DIM = 64
GROUP = 256
HEADS = GROUP // HEAD_DIM
RWKV_COLS = 3 * GROUP + 32 + 32 + 64
RWKV_LR = RWKV_COLS - 3 * GROUP
RWKV_GN_EPS = 64e-5
LRU_C = 8.0
MOBA_BLOCK = 256
MOBA_TOPK = 3
ROPE_THETA = 10000.0
NEG_INF = -1e30
GN_EPS = 1e-5
D_FF = 2816
LN_EPS = 1e-5
RET_CHUNK = 256

VMEM_LIMIT_BYTES = 56 * 1024 * 1024
SUBLANES = 8
LANES = 128


def _cparams(*semantics):
    return pltpu.CompilerParams(dimension_semantics=semantics, vmem_limit_bytes=VMEM_LIMIT_BYTES)


def _hdot(a, b):
    return jnp.dot(a, b, precision=HIGHEST, preferred_element_type=F32)


def _bdot(a, b):
    return jnp.dot(a.astype(BF16), b.astype(BF16), preferred_element_type=F32)


def _bdot_t(a, b):
    return lax.dot_general(a.astype(BF16), b.astype(BF16), (((1,), (1,)), ((), ())),
                           preferred_element_type=F32)


def _softplus(z):
    return jnp.maximum(z, 0.0) + jnp.log1p(jnp.exp(-jnp.abs(z)))


def _gelu_tanh(x):
    return 0.5 * x * (1.0 + jnp.tanh(math.sqrt(2.0 / math.pi) * (x + 0.044715 * (x * x * x))))


def _shift_rows(x, halo, s):
    rolled = pltpu.roll(x, s, axis=0)
    row8 = lax.broadcasted_iota(jnp.int32, halo.shape, 0)
    top = jnp.where(row8 < s, pltpu.roll(halo, s, axis=0), rolled[0:SUBLANES])
    return jnp.concatenate([top, rolled[SUBLANES:]], axis=0)


def _layer_norm(y, g, b):
    mu = jnp.mean(y, axis=-1, keepdims=True)
    d = y - mu
    var = jnp.mean(d * d, axis=-1, keepdims=True)
    return d * lax.rsqrt(var + LN_EPS) * g + b


def _halo_map(tm, rows=SUBLANES):
    return lambda i: (jnp.maximum(i * (tm // rows) - 1, 0), 0)


def _split3(x):
    hi = x.astype(BF16)
    r = x - hi.astype(F32)
    mid = r.astype(BF16)
    lo = (r - mid.astype(F32)).astype(BF16)
    return hi, mid, lo


def _seg_sum(x, ones_blk):
    return sum(jnp.dot(part, ones_blk, preferred_element_type=F32) for part in _split3(x))


def _dot3(a, b):
    a_hi, a_lo, _ = _split3(a)
    b_hi, b_lo, _ = _split3(b)
    dot = functools.partial(jnp.dot, preferred_element_type=F32)
    return dot(a_hi, b_hi) + (dot(a_hi, b_lo) + dot(a_lo, b_hi))


IN_WIDTHS = (RWKV_COLS, 2 * GROUP, 3 * GROUP, 4 * GROUP)
IN_COLS = sum(IN_WIDTHS)


def _in_proj_kernel(x_ref, w_ref, *outs):
    xb = x_ref[...].astype(BF16)
    col = 0
    for o in outs:
        n = o.shape[1]
        o[...] = jnp.dot(xb, w_ref[:, col:col + n], preferred_element_type=F32)
        col += n


def _in_proj(x2d, w_bf16, tm=512):
    m = x2d.shape[0]
    return pl.pallas_call(
        _in_proj_kernel,
        grid=(m // tm,),
        in_specs=[pl.BlockSpec((tm, D_MODEL), lambda i: (i, 0)),
                  pl.BlockSpec((D_MODEL, IN_COLS), lambda i: (0, 0))],
        out_specs=[pl.BlockSpec((tm, w), lambda i: (i, 0)) for w in IN_WIDTHS],
        out_shape=[jax.ShapeDtypeStruct((m, w), F32) for w in IN_WIDTHS],
        compiler_params=_cparams("arbitrary"),
        name="in_proj",
    )(x2d, w_bf16)


def _rows_to_lanes(x_s, out_ref, t0, nb, tb, gw, dup):
    steps = LANES // gw
    lane_grp = lax.broadcasted_iota(jnp.int32, (nb, LANES), 1) // gw
    z = [None] * (GROUP // gw)
    for tt in range(steps):
        xt = [x_s[c, pl.ds(t0 + tt, nb, stride=tb), :] for c in range(GROUP // LANES)]
        for g in range(len(z)):
            half = xt[g // steps]
            shift = (tt - g % steps) % steps * gw
            piece = pltpu.roll(half, shift, axis=1) if shift else half
            z[g] = piece if tt == 0 else jnp.where(lane_grp == tt, piece, z[g])
    tile = jnp.concatenate([zg for zg in z for _ in range(dup)], axis=0)
    out_ref[pl.ds(t0, steps)] = tile.T.reshape(steps, gw, tile.shape[0])


def _lanes_to_rows(y_ref, y_s, t0, nb, tb, gw):
    steps = LANES // gw
    tile = y_ref[pl.ds(t0, steps)]
    tile = tile.reshape(steps * gw, tile.shape[2]).T
    lane_grp = lax.broadcasted_iota(jnp.int32, (nb, LANES), 1) // gw
    for tt in range(steps):
        for half in range(GROUP // LANES):
            cols = None
            for pos in range(steps):
                g = half * steps + pos
                piece = tile[g * nb:(g + 1) * nb, :]
                shift = (pos - tt) % steps * gw
                piece = pltpu.roll(piece, shift, axis=1) if shift else piece
                cols = piece if pos == 0 else jnp.where(lane_grp == pos, piece, cols)
            y_s[half, pl.ds(t0 + tt, nb, stride=tb), :] = cols


def _rwkv_prep_kernel(p_ref, halo_ref, mu_ref, pv_ref, lrw_ref, ones_ref,
                      r_o, k_o, v_o, kk_o, b_o, w_o, g_o, bv_o, *stage, nb, tb):
    rows = nb * tb
    pa = p_ref[...].reshape(rows, RWKV_COLS)
    prev = jnp.where(pl.program_id(0) == 0, 0.0, halo_ref[:, SUBLANES - 1:SUBLANES, :])
    prev = jnp.broadcast_to(prev, (nb, tb, RWKV_COLS)).reshape(rows, RWKV_COLS)
    t_idx = lax.broadcasted_iota(jnp.int32, pa.shape, 0) % tb
    shifted = jnp.where(t_idx == 0, prev, pltpu.roll(pa, 1, axis=0))
    pa = pa + (shifted - pa) * mu_ref[...]
    r = pa[:, 0:GROUP]
    k = pa[:, GROUP:2 * GROUP]
    v = pa[:, 2 * GROUP:3 * GROUP]
    lr = pa[:, 3 * GROUP:RWKV_COLS]
    w0, a0, k_k, k_a, r_k = (pv_ref[n:n + 1, :] for n in range(5))
    tw = _dot3(jnp.tanh(lr), lrw_ref[0])
    ta = _dot3(lr, lrw_ref[1])
    g = _dot3(jax.nn.sigmoid(lr), lrw_ref[2])
    w_log = -_softplus(-(w0 + tw)) - 0.5
    decay = jnp.exp(-jnp.exp(w_log))
    a = jax.nn.sigmoid(a0 + ta)
    kk = k * k_k
    k2 = k * (1.0 + (a - 1.0) * k_a)
    ss = _seg_sum(kk * kk, ones_ref[...])
    kkn = kk / jnp.maximum(jnp.sqrt(ss), 1e-12)
    bonus = _seg_sum(r * k2 * r_k, ones_ref[...])
    g_o[...] = g.reshape(nb, tb, GROUP)
    bv_o[...] = (bonus * v).reshape(nb, tb, GROUP)
    layouts = ((r, r_o, HEAD_DIM, 2), (k2, k_o, HEAD_DIM, 2), (v, v_o, HEAD_DIM // 2, 1),
               (kkn, kk_o, HEAD_DIM, 2), (kkn * a, b_o, HEAD_DIM, 2), (decay, w_o, HEAD_DIM, 2))
    for (x, _, _, _), x_s in zip(layouts, stage):
        for c in range(GROUP // LANES):
            x_s[c] = x[:, c * LANES:(c + 1) * LANES]
    span = LANES // (HEAD_DIM // 2)

    def move(n, carry):
        for (_, out, gw, dup), x_s in zip(layouts, stage):
            for t0 in range(0, span, LANES // gw):
                _rows_to_lanes(x_s, out, n * span + t0, nb, tb, gw, dup)
        return carry

    lax.fori_loop(0, tb // span, move, 0)


RWKV_TB = 32


def _rwkv_prep(p_rwkv, mu, pv, lrw, ones_blk, nb, seq, tb=RWKV_TB):
    lanes = 2 * HEADS * nb
    full = lambda a: pl.BlockSpec(a.shape, lambda i: (0,) * a.ndim)
    key_spec = pl.BlockSpec((tb, HEAD_DIM, lanes), lambda i: (i, 0, 0))
    val_spec = pl.BlockSpec((tb, HEAD_DIM // 2, lanes), lambda i: (i, 0, 0))
    nat_spec = pl.BlockSpec((nb, tb, GROUP), lambda i: (0, i, 0))
    key_shape = jax.ShapeDtypeStruct((seq, HEAD_DIM, lanes), F32)
    val_shape = jax.ShapeDtypeStruct((seq, HEAD_DIM // 2, lanes), F32)
    nat_shape = jax.ShapeDtypeStruct((nb, seq, GROUP), F32)
    return pl.pallas_call(
        functools.partial(_rwkv_prep_kernel, nb=nb, tb=tb),
        grid=(seq // tb,),
        in_specs=[pl.BlockSpec((nb, tb, RWKV_COLS), lambda i: (0, i, 0)),
                  pl.BlockSpec((nb, SUBLANES, RWKV_COLS),
                               lambda i: (0, jnp.maximum(i * (tb // SUBLANES) - 1, 0), 0)),
                  full(mu), full(pv), full(lrw), full(ones_blk)],
        out_specs=[key_spec, key_spec, val_spec, key_spec, key_spec, key_spec, nat_spec, nat_spec],
        out_shape=[key_shape, key_shape, val_shape, key_shape, key_shape, key_shape, nat_shape, nat_shape],
        scratch_shapes=[pltpu.VMEM((GROUP // LANES, nb * tb, LANES), F32)] * 6,
        compiler_params=_cparams("arbitrary"),
        name="rwkv_prep",
    )(p_rwkv, p_rwkv, mu, pv, lrw, ones_blk)


def _rwkv_scan_kernel(r_ref, k_ref, v_ref, kk_ref, b_ref, w_ref, y_ref, s_ref, *, tb):
    n_j = s_ref.shape[0]
    tile = s_ref.shape[1:]

    @pl.when(pl.program_id(0) == 0)
    def _():
        s_ref[...] = jnp.zeros_like(s_ref)

    sa0 = jnp.zeros(tile, F32)
    for j in range(n_j):
        sa0 = sa0 + s_ref[j] * kk_ref[0, pl.ds(j, 1), :]

    def step(t, sa):
        t_next = jnp.minimum(t + 1, tb - 1)
        vt = v_ref[t]
        yacc = jnp.zeros(tile, F32)
        nacc = jnp.zeros(tile, F32)
        for j in range(n_j):
            sj = (s_ref[j] * w_ref[t, pl.ds(j, 1), :] - sa * b_ref[t, pl.ds(j, 1), :]
                  + vt * k_ref[t, pl.ds(j, 1), :])
            s_ref[j] = sj
            yacc = yacc + sj * r_ref[t, pl.ds(j, 1), :]
            nacc = nacc + sj * kk_ref[t_next, pl.ds(j, 1), :]
        y_ref[t] = yacc
        return nacc

    lax.fori_loop(0, tb, step, sa0)


def _rwkv_scan(r, k, v, kk, b, w, tb=64):
    seq, n_j, lanes = r.shape
    kspec = pl.BlockSpec((tb, n_j, lanes), lambda i: (i, 0, 0))
    vspec = pl.BlockSpec((tb,) + v.shape[1:], lambda i: (i, 0, 0))
    return pl.pallas_call(
        functools.partial(_rwkv_scan_kernel, tb=tb),
        grid=(seq // tb,),
        in_specs=[kspec, kspec, vspec, kspec, kspec, kspec],
        out_specs=vspec,
        out_shape=jax.ShapeDtypeStruct(v.shape, F32),
        scratch_shapes=[pltpu.VMEM((n_j,) + v.shape[1:], F32)],
        compiler_params=_cparams("arbitrary"),
        name="rwkv_scan",
    )(r, k, v, kk, b, w)


def _rwkv_post_kernel(y_ref, g_ref, bv_ref, gn_ref, ones_ref, o_ref, y_s, *, nb, tb):
    gw = y_ref.shape[1]
    steps = LANES // gw
    unroll = 2

    def move(n, carry):
        for u in range(unroll):
            _lanes_to_rows(y_ref, y_s, (n * unroll + u) * steps, nb, tb, gw)
        return carry

    lax.fori_loop(0, tb // (steps * unroll), move, 0)
    y = jnp.concatenate([y_s[c] for c in range(GROUP // LANES)], axis=1)
    inv_n = 1.0 / HEAD_DIM
    mu = _seg_sum(y, ones_ref[...]) * inv_n
    d = y - mu
    var = _seg_sum(d * d, ones_ref[...]) * inv_n
    yn = d * lax.rsqrt(var + RWKV_GN_EPS) * gn_ref[0:1, :] + gn_ref[1:2, :]
    out = (yn + bv_ref[...].reshape(nb * tb, GROUP)) * g_ref[...].reshape(nb * tb, GROUP)
    o_ref[...] = out.reshape(nb, tb, GROUP)


def _rwkv_post(y, g, bv, gn, ones_blk, nb, seq, tb=RWKV_TB):
    nat_spec = pl.BlockSpec((nb, tb, GROUP), lambda i: (0, i, 0))
    full = lambda a: pl.BlockSpec(a.shape, lambda i: (0,) * a.ndim)
    return pl.pallas_call(
        functools.partial(_rwkv_post_kernel, nb=nb, tb=tb),
        grid=(seq // tb,),
        in_specs=[pl.BlockSpec((tb,) + y.shape[1:], lambda i: (i, 0, 0)), nat_spec, nat_spec,
                  full(gn), full(ones_blk)],
        out_specs=nat_spec,
        out_shape=jax.ShapeDtypeStruct((nb, seq, GROUP), F32),
        scratch_shapes=[pltpu.VMEM((GROUP // LANES, nb * tb, LANES), F32)],
        compiler_params=_cparams("arbitrary"),
        name="rwkv_post",
    )(y, g, bv, gn, ones_blk)


def _lru_kernel(p_ref, halo_ref, cw_ref, pv_ref, wr_ref, wi_ref, o_ref, hc_ref, *, tt):
    j = pl.program_id(1)
    x = p_ref[:, 0:GROUP]
    gb = p_ref[:, GROUP:2 * GROUP]
    halo = jnp.where(j == 0, 0.0, halo_ref[:, 0:GROUP])
    conv_b, b_r, b_i, lam = (pv_ref[n:n + 1, :] for n in range(4))
    xc = x * cw_ref[3:4, :] + conv_b
    for s in (1, 2, 3):
        xc = xc + _shift_rows(x, halo, s) * cw_ref[3 - s:4 - s, :]
    gate_r = jax.nn.sigmoid(_bdot(xc, wr_ref[...]) + b_r)
    gate_i = jax.nn.sigmoid(_bdot(xc, wi_ref[...]) + b_i)
    log_a = -LRU_C * gate_r * _softplus(-lam)
    z = 2.0 * log_a
    expm1_z = jnp.tanh(0.5 * z) * (jnp.exp(z) + 1.0)
    a = jnp.exp(log_a)
    u = jnp.sqrt(-expm1_z) * (gate_i * xc)

    @pl.when(j == 0)
    def _():
        hc_ref[...] = jnp.zeros_like(hc_ref)

    row = lax.broadcasted_iota(jnp.int32, a.shape, 0)
    d = 1
    while d < tt:
        keep = row >= d
        u = a * jnp.where(keep, pltpu.roll(u, d, axis=0), 0.0) + u
        a = a * jnp.where(keep, pltpu.roll(a, d, axis=0), 1.0)
        d *= 2
    h = a * hc_ref[...] + u
    hc_ref[...] = h[tt - 1:tt, :]
    o_ref[...] = _gelu_tanh(gb) * h


def _lru(p_lru, cw, pv, wr, wi, nb, seq, tt=256):
    m = p_lru.shape[0]
    nt = seq // tt
    full = lambda a: pl.BlockSpec(a.shape, lambda b, j: (0,) * a.ndim)
    halo = _halo_map(tt)
    return pl.pallas_call(
        functools.partial(_lru_kernel, tt=tt),
        grid=(nb, nt),
        in_specs=[pl.BlockSpec((tt, 2 * GROUP), lambda b, j: (b * nt + j, 0)),
                  pl.BlockSpec((SUBLANES, 2 * GROUP), lambda b, j: halo(b * nt + j)),
                  full(cw), full(pv), full(wr), full(wi)],
        out_specs=pl.BlockSpec((tt, GROUP), lambda b, j: (b * nt + j, 0)),
        out_shape=jax.ShapeDtypeStruct((m, GROUP), F32),
        scratch_shapes=[pltpu.VMEM((1, GROUP), F32)],
        compiler_params=_cparams("arbitrary", "arbitrary"),
        name="lru",
    )(p_lru, p_lru, cw, pv, wr, wi)


def _rope_rows(x, cos, sin_signed):
    lane = lax.broadcasted_iota(jnp.int32, x.shape, 1)
    first_half = lane % HEAD_DIM < HEAD_DIM // 2
    width = x.shape[1]
    partner = jnp.where(first_half, pltpu.roll(x, width - HEAD_DIM // 2, axis=1),
                        pltpu.roll(x, HEAD_DIM // 2, axis=1))
    return x * cos + partner * sin_signed


def _rope_moba_kernel(p_ref, cos_ref, sin_ref, qt_o, k_o, vt_o, km_o):
    cos = cos_ref[...]
    sin = sin_ref[...]
    q = _rope_rows(p_ref[:, 0:GROUP], cos, sin) * HEAD_DIM ** -0.5
    qt_o[...] = q.T
    k = _rope_rows(p_ref[:, GROUP:2 * GROUP], cos, sin)
    k_o[...] = jnp.zeros_like(k_o)
    for h in range(HEADS):
        k_o[:, h * LANES:h * LANES + HEAD_DIM] = k[:, h * HEAD_DIM:(h + 1) * HEAD_DIM].astype(BF16)
    km_o[0] = jnp.mean(k, axis=0, keepdims=True)
    vt_o[...] = p_ref[:, 2 * GROUP:3 * GROUP].T.astype(BF16)


def _rope_moba(p, cos, sin, seq):
    m = p.shape[0]
    tm = MOBA_BLOCK
    nt = seq // tm
    tab = pl.BlockSpec((tm, GROUP), lambda j, b: (j, 0))
    col = pl.BlockSpec((GROUP, tm), lambda j, b: (0, b * nt + j))
    return pl.pallas_call(
        _rope_moba_kernel,
        grid=(nt, m // seq),
        in_specs=[pl.BlockSpec((tm, 3 * GROUP), lambda j, b: (b * nt + j, 0)), tab, tab],
        out_specs=[col, pl.BlockSpec((tm, HEADS * LANES), lambda j, b: (b * nt + j, 0)), col,
                   pl.BlockSpec((1, 1, GROUP), lambda j, b: (b * nt + j, 0, 0))],
        out_shape=[jax.ShapeDtypeStruct((GROUP, m), F32), jax.ShapeDtypeStruct((m, HEADS * LANES), BF16),
                   jax.ShapeDtypeStruct((GROUP, m), BF16), jax.ShapeDtypeStruct((m // tm, 1, GROUP), F32)],
        compiler_params=_cparams("arbitrary", "arbitrary"),
        name="rope_moba",
    )(p, cos, sin)


def _rope_tables(seq):
    inv = ROPE_THETA ** (-jnp.arange(0, HEAD_DIM, 2, dtype=F32) / HEAD_DIM)
    ang = jnp.arange(seq, dtype=F32)[:, None] * inv[None, :]
    cos, sin = jnp.cos(ang), jnp.sin(ang)
    cos_full = jnp.tile(jnp.concatenate([cos, cos], axis=1), (1, HEADS))
    sin_signed = jnp.tile(jnp.concatenate([-sin, sin], axis=1), (1, HEADS))
    return cos_full, sin_signed


def _moba_kernel(qt_ref, k_ref, vt_ref, km_ref, o_ref, m_s, l_s, acc_s, s_s, p_s):
    blk = pl.program_id(1)
    tq = qt_ref.shape[1]
    n_blocks = km_ref.shape[0]
    key_i = lax.broadcasted_iota(jnp.int32, (MOBA_BLOCK, tq), 0)
    qry_i = lax.broadcasted_iota(jnp.int32, (MOBA_BLOCK, tq), 1)
    blk_id = lax.broadcasted_iota(jnp.int32, (n_blocks, tq), 0)
    valid = blk_id < blk
    heads = [slice(h * HEAD_DIM, (h + 1) * HEAD_DIM) for h in range(HEADS)]

    sels = []
    for hs in heads:
        gate = _hdot(km_ref[:, hs], qt_ref[hs, :])
        sel = jnp.zeros((n_blocks, tq), F32)
        for n in range(n_blocks):
            gn = gate[n:n + 1, :]
            beats = valid & ((gate > gn) | ((gate == gn) & (blk_id < n)))
            cnt = jnp.sum(beats.astype(F32), axis=0, keepdims=True)
            sel = jnp.where((blk_id == n) & (cnt < MOBA_TOPK) & valid, 1.0, sel)
        sels.append(sel)

    def attend(start, keep_fn, first):
        for h, hs in enumerate(heads):
            qt = jnp.concatenate([qt_ref[hs, :], jnp.zeros((LANES - HEAD_DIM, tq), F32)], axis=0)
            s_s[h] = jnp.dot(k_ref[pl.ds(start, MOBA_BLOCK), h * LANES:(h + 1) * LANES], qt.astype(BF16),
                             preferred_element_type=F32)
        alphas = []
        for h in range(HEADS):
            s = jnp.where(keep_fn(h), s_s[h], NEG_INF)
            m_blk = jnp.max(s, axis=0, keepdims=True)
            if first:
                m_new = m_blk
            else:
                m_new = jnp.maximum(m_s[h], m_blk)
                alphas.append(jnp.exp(m_s[h] - m_new))
            p = jnp.exp(s - m_new)
            p_s[h] = p.astype(BF16)
            p_sum = jnp.sum(p, axis=0, keepdims=True)
            l_s[h] = p_sum if first else alphas[h] * l_s[h] + p_sum
            m_s[h] = m_new
        for h, hs in enumerate(heads):
            pv = jnp.dot(vt_ref[hs, pl.ds(start, MOBA_BLOCK)], p_s[h], preferred_element_type=F32)
            acc_s[hs, :] = pv if first else alphas[h] * acc_s[hs, :] + pv

    causal = key_i <= qry_i
    attend(pl.multiple_of(blk * MOBA_BLOCK, MOBA_BLOCK), lambda h: causal, True)

    def body(n, carry):
        picked = lambda h: jnp.sum(jnp.where(blk_id == n, sels[h], 0.0), axis=0, keepdims=True) > 0.5
        attend(pl.multiple_of(n * MOBA_BLOCK, MOBA_BLOCK), picked, False)
        return carry

    lax.fori_loop(0, blk, body, 0)
    out_t = jnp.concatenate([acc_s[hs, :] / l_s[h] for h, hs in enumerate(heads)], axis=0)
    o_ref[...] = out_t.T


def _moba(qt, kh, vt, kmean, nb, seq):
    m = qt.shape[1]
    nblk = seq // MOBA_BLOCK
    return pl.pallas_call(
        _moba_kernel,
        grid=(nb, nblk),
        in_specs=[pl.BlockSpec((GROUP, MOBA_BLOCK), lambda b, j: (0, b * nblk + j)),
                  pl.BlockSpec((seq, HEADS * LANES), lambda b, j: (b, 0)),
                  pl.BlockSpec((GROUP, seq), lambda b, j: (0, b)),
                  pl.BlockSpec((None, nblk, GROUP), lambda b, j: (b, 0, 0))],
        out_specs=pl.BlockSpec((MOBA_BLOCK, GROUP), lambda b, j: (b * nblk + j, 0)),
        out_shape=jax.ShapeDtypeStruct((m, GROUP), F32),
        scratch_shapes=[pltpu.VMEM((HEADS, 1, MOBA_BLOCK), F32), pltpu.VMEM((HEADS, 1, MOBA_BLOCK), F32),
                        pltpu.VMEM((GROUP, MOBA_BLOCK), F32),
                        pltpu.VMEM((HEADS, MOBA_BLOCK, MOBA_BLOCK), F32),
                        pltpu.VMEM((HEADS, MOBA_BLOCK, MOBA_BLOCK), BF16)],
        compiler_params=_cparams("arbitrary", "arbitrary"),
        name="moba",
    )(qt, kh, vt, kmean.reshape(nb, nblk, GROUP))


def _ret_kernel(p_ref, cos_ref, sin_ref, gn_ref, o_ref, st_ref, mask_s):
    lc = p_ref.shape[0]
    log_gammas = [math.log1p(-(2.0 ** (-5.0 - h))) for h in range(HEADS)]

    @pl.when(pl.program_id(1) == 0)
    def _():
        st_ref[...] = jnp.zeros_like(st_ref)

    @pl.when((pl.program_id(0) == 0) & (pl.program_id(1) == 0))
    def _():
        ri = lax.broadcasted_iota(jnp.int32, (lc, lc), 0)
        ci = lax.broadcasted_iota(jnp.int32, (lc, lc), 1)
        diff = (ri - ci).astype(F32)
        for h in range(HEADS):
            mask_s[h] = jnp.where(diff >= 0, jnp.exp(log_gammas[h] * jnp.maximum(diff, 0.0)), 0.0)

    cos = cos_ref[...]
    sin = sin_ref[...]
    q_all = _rope_rows(p_ref[:, 0:GROUP], cos, sin)
    k_all = _rope_rows(p_ref[:, GROUP:2 * GROUP], cos, sin) * HEAD_DIM ** -0.5
    pos = lax.broadcasted_iota(jnp.int32, (lc, 1), 0).astype(F32)
    for h in range(HEADS):
        hs = slice(h * HEAD_DIM, (h + 1) * HEAD_DIM)
        log_gamma = log_gammas[h]
        q = q_all[:, hs]
        k = k_all[:, hs]
        v = p_ref[:, 2 * GROUP + h * HEAD_DIM:2 * GROUP + (h + 1) * HEAD_DIM]
        g = p_ref[:, 3 * GROUP + h * HEAD_DIM:3 * GROUP + (h + 1) * HEAD_DIM]
        intra = _bdot_t(q, k) * mask_s[h]
        state = st_ref[h]
        out = _bdot(intra, v) + _bdot(q * jnp.exp(log_gamma * (pos + 1.0)), state)
        k_w = jnp.exp(log_gamma * (lc - 1.0 - pos))
        kv = lax.dot_general((k * k_w).astype(BF16), v.astype(BF16), (((0,), (0,)), ((), ())),
                             preferred_element_type=F32)
        st_ref[h] = state * math.exp(log_gamma * lc) + kv
        mu = jnp.mean(out, axis=-1, keepdims=True)
        d = out - mu
        var = jnp.mean(d * d, axis=-1, keepdims=True)
        yn = d * lax.rsqrt(var + GN_EPS) * gn_ref[0:1, hs] + gn_ref[1:2, hs]
        o_ref[:, hs] = g * jax.nn.sigmoid(g) * yn


def _retention(p_ret, cos, sin, gn, nb, seq):
    m = p_ret.shape[0]
    lc = RET_CHUNK
    nt = seq // lc
    tab = pl.BlockSpec((lc, GROUP), lambda b, j: (j, 0))
    return pl.pallas_call(
        _ret_kernel,
        grid=(nb, nt),
        in_specs=[pl.BlockSpec((lc, 4 * GROUP), lambda b, j: (b * nt + j, 0)), tab, tab,
                  pl.BlockSpec(gn.shape, lambda b, j: (0, 0))],
        out_specs=pl.BlockSpec((lc, GROUP), lambda b, j: (b * nt + j, 0)),
        out_shape=jax.ShapeDtypeStruct((m, GROUP), F32),
        scratch_shapes=[pltpu.VMEM((HEADS, HEAD_DIM, HEAD_DIM), F32), pltpu.VMEM((HEADS, lc, lc), F32)],
        compiler_params=_cparams("arbitrary", "arbitrary"),
        name="retention",
    )(p_ret, cos, sin, gn)


def _out_proj_kernel(ya_ref, yb_ref, yc_ref, yd_ref, x_ref, w_ref, ln_ref, o_ref, *, alpha):
    y = jnp.concatenate([r[...].astype(BF16) for r in (ya_ref, yb_ref, yc_ref, yd_ref)], axis=1)
    acc = alpha * x_ref[...] + jnp.dot(y, w_ref[...], preferred_element_type=F32)
    o_ref[...] = _layer_norm(acc, ln_ref[0:1, :], ln_ref[1:2, :])


def _out_proj(ys, x2d, w_bf16, ln, alpha, tm=512):
    m = x2d.shape[0]
    yspec = pl.BlockSpec((tm, GROUP), lambda i: (i, 0))
    xspec = pl.BlockSpec((tm, D_MODEL), lambda i: (i, 0))
    full = lambda a: pl.BlockSpec(a.shape, lambda i: (0,) * a.ndim)
    return pl.pallas_call(
        functools.partial(_out_proj_kernel, alpha=alpha),
        grid=(m // tm,),
        in_specs=[yspec] * 4 + [xspec, full(w_bf16), full(ln)],
        out_specs=xspec,
        out_shape=jax.ShapeDtypeStruct((m, D_MODEL), F32),
        compiler_params=_cparams("arbitrary"),
        name="out_proj_ln",
    )(*ys, x2d, w_bf16, ln)


FFN_TN = 256


def _ffn_up_kernel(x_ref, w_ref, cw_ref, cb_ref, o_ref, tail_ref, h_s, *, tiles_per_seq):
    xb = x_ref[...].astype(BF16)
    tm = xb.shape[0]
    first = pl.program_id(0) % tiles_per_seq == 0

    @pl.when(pl.program_id(0) == 0)
    def _():
        tail_ref[...] = jnp.zeros_like(tail_ref)

    def conv_cols(c0, slot):
        h = jnp.dot(xb, w_ref[:, c0:c0 + FFN_TN], preferred_element_type=F32)
        h_s[slot, 0:SUBLANES, :] = jnp.where(first, 0.0, tail_ref[:, c0:c0 + FFN_TN])
        h_s[slot, SUBLANES:, :] = h
        tail_ref[:, c0:c0 + FFN_TN] = h[tm - SUBLANES:tm]
        u = h * cw_ref[2:3, c0:c0 + FFN_TN] + cb_ref[:, c0:c0 + FFN_TN]
        for s in (1, 2):
            u = u + h_s[slot, pl.ds(SUBLANES - s, tm), :] * cw_ref[2 - s:3 - s, c0:c0 + FFN_TN]
        return u

    for c in range(D_FF // FFN_TN):
        gate = conv_cols(c * FFN_TN, 0)
        val = conv_cols(D_FF + c * FFN_TN, 1)
        o_ref[:, c * FFN_TN:(c + 1) * FFN_TN] = (_gelu_tanh(gate) * val).astype(BF16)


def _ffn_up(x2d, w_bf16, cw, cb, seq, tm=512):
    m = x2d.shape[0]
    full = lambda a: pl.BlockSpec(a.shape, lambda i: (0,) * a.ndim)
    return pl.pallas_call(
        functools.partial(_ffn_up_kernel, tiles_per_seq=seq // tm),
        grid=(m // tm,),
        in_specs=[pl.BlockSpec((tm, D_MODEL), lambda i: (i, 0)), full(w_bf16), full(cw), full(cb)],
        out_specs=pl.BlockSpec((tm, D_FF), lambda i: (i, 0)),
        out_shape=jax.ShapeDtypeStruct((m, D_FF), BF16),
        scratch_shapes=[pltpu.VMEM((SUBLANES, 2 * D_FF), F32), pltpu.VMEM((2, SUBLANES + tm, FFN_TN), F32)],
        compiler_params=_cparams("arbitrary"),
        name="ffn_up",
    )(x2d, w_bf16, cw, cb)


def _ffn_down_kernel(a_ref, x_ref, w_ref, ln_ref, o_ref, *, alpha):
    acc = alpha * x_ref[...] + jnp.dot(a_ref[...], w_ref[...], preferred_element_type=F32)
    o_ref[...] = _layer_norm(acc, ln_ref[0:1, :], ln_ref[1:2, :])


def _ffn_down(act, x2d, w_bf16, ln, alpha, tm=512):
    m = x2d.shape[0]
    xspec = pl.BlockSpec((tm, D_MODEL), lambda i: (i, 0))
    full = lambda a: pl.BlockSpec(a.shape, lambda i: (0,) * a.ndim)
    return pl.pallas_call(
        functools.partial(_ffn_down_kernel, alpha=alpha),
        grid=(m // tm,),
        in_specs=[pl.BlockSpec((tm, D_FF), lambda i: (i, 0)), xspec, full(w_bf16), full(ln)],
        out_specs=xspec,
        out_shape=jax.ShapeDtypeStruct((m, D_MODEL), F32),
        compiler_params=_cparams("arbitrary"),
        name="ffn_down_ln",
    )(act, x2d, w_bf16, ln)


def _block_diag(w):
    h, n, _ = w.shape
    eye = jnp.eye(h, dtype=w.dtype)
    return (eye[:, None, :, None] * w[:, :, None, :]).reshape(h * n, h * n)


def _pad_rows(w, offset, total):
    return jnp.zeros((total, w.shape[1]), w.dtype).at[offset:offset + w.shape[0]].set(w)


def _rows8(*vecs):
    rows = [v.reshape(1, -1) for v in vecs]
    rows += [jnp.zeros_like(rows[0])] * (SUBLANES - len(rows))
    return jnp.concatenate(rows, axis=0)


def kernel(x, w_in, tshift_mu, rwkv_w0, rwkv_w_up, rwkv_a0, rwkv_a_up, rwkv_g_up, rwkv_k_k, rwkv_k_a, rwkv_r_k, rwkv_gn_g, rwkv_gn_b, lru_conv_w, lru_conv_b, lru_w_r, lru_b_r, lru_w_i, lru_b_i, lru_lambda, ret_gn_g, ret_gn_b, w_out, ln1_g, ln1_b, ffn_w_up, ffn_conv_w, ffn_conv_b, ffn_w_down, ln2_g, ln2_b):
    nb, seq, _ = x.shape
    depth = w_in.shape[0]
    alpha = (2.0 * depth) ** 0.25
    cos, sin = _rope_tables(seq)
    ones_blk = _block_diag(jnp.ones((HEADS, HEAD_DIM, HEAD_DIM), F32)).astype(BF16)
    x2d = x.reshape(nb * seq, D_MODEL)
    for l in range(depth):
        p_rwkv, p_lru, p_att, p_ret = _in_proj(x2d, w_in[l].astype(BF16))

        lrw = jnp.stack([_pad_rows(rwkv_w_up[l], 0, RWKV_LR), _pad_rows(rwkv_a_up[l], 32, RWKV_LR),
                         _pad_rows(rwkv_g_up[l], 64, RWKV_LR)])
        pv = _rows8(rwkv_w0[l], rwkv_a0[l], rwkv_k_k[l], rwkv_k_a[l], rwkv_r_k[l])
        *scan_ops, g, bv = _rwkv_prep(p_rwkv.reshape(nb, seq, RWKV_COLS), tshift_mu[l].reshape(1, -1), pv, lrw,
                                      ones_blk, nb, seq)
        y = _rwkv_scan(*scan_ops)
        y_a = _rwkv_post(y, g, bv, _rows8(rwkv_gn_g[l], rwkv_gn_b[l]), ones_blk, nb, seq)
        y_a = y_a.reshape(nb * seq, GROUP)

        y_b = _lru(p_lru, lru_conv_w[l], _rows8(lru_conv_b[l], lru_b_r[l], lru_b_i[l], lru_lambda[l]),
                   _block_diag(lru_w_r[l]).astype(BF16), _block_diag(lru_w_i[l]).astype(BF16), nb, seq)

        y_c = _moba(*_rope_moba(p_att, cos, sin, seq), nb, seq)

        y_d = _retention(p_ret, cos, sin, _rows8(ret_gn_g[l], ret_gn_b[l]), nb, seq)

        x2d = _out_proj((y_a, y_b, y_c, y_d), x2d, w_out[l].astype(BF16), _rows8(ln1_g[l], ln1_b[l]), alpha)
        act = _ffn_up(x2d, ffn_w_up[l].astype(BF16), ffn_conv_w[l], ffn_conv_b[l].reshape(1, -1), seq)
        x2d = _ffn_down(act, x2d, ffn_w_down[l].astype(BF16), _rows8(ln2_g[l], ln2_b[l]), alpha)
    return x2d.reshape(nb, seq, D_MODEL)
```

```python
import functools
import math

import jax
import jax.numpy as jnp
from jax import lax
from jax.experimental import pallas as pl
from jax.experimental.pallas import tpu as pltpu

F32 = jnp.float32
BF16 = jnp.bfloat16
HIGHEST = lax.Precision.HIGHEST

D_MODEL = 1024
HEAD_DIM = 64
GROUP = 256
HEADS = GROUP // HEAD_DIM
RWKV_COLS = 3 * GROUP + 32 + 32 + 64
RWKV_LR = RWKV_COLS - 3 * GROUP
RWKV_GN_EPS = 64e-5
LRU_C = 8.0
MOBA_BLOCK = 256
MOBA_TOPK = 3
ROPE_THETA = 10000.0
NEG_INF = -1e30
GN_EPS = 1e-5
D_FF = 2816
LN_EPS = 1e-5
RET_CHUNK = 256

VMEM_LIMIT_BYTES = 56 * 1024 * 1024
SUBLANES = 8
LANES = 128


def _cparams(*semantics):
    return pltpu.CompilerParams(dimension_semantics=semantics, vmem_limit_bytes=VMEM_LIMIT_BYTES)


def _hdot(a, b):
    return jnp.dot(a, b, precision=HIGHEST, preferred_element_type=F32)


def _bdot(a, b):
    return jnp.dot(a.astype(BF16), b.astype(BF16), preferred_element_type=F32)


def _bdot_t(a, b):
    return lax.dot_general(a.astype(BF16), b.astype(BF16), (((1,), (1,)), ((), ())),
                           preferred_element_type=F32)


def _softplus(z):
    return jnp.maximum(z, 0.0) + jnp.log1p(jnp.exp(-jnp.abs(z)))


def _gelu_tanh(x):
    return 0.5 * x * (1.0 + jnp.tanh(math.sqrt(2.0 / math.pi) * (x + 0.044715 * (x * x * x))))


def _shift_rows(x, halo, s):
    rolled = pltpu.roll(x, s, axis=0)
    row8 = lax.broadcasted_iota(jnp.int32, halo.shape, 0)
    top = jnp.where(row8 < s, pltpu.roll(halo, s, axis=0), rolled[0:SUBLANES])
    return jnp.concatenate([top, rolled[SUBLANES:]], axis=0)


def _layer_norm(y, g, b):
    mu = jnp.mean(y, axis=-1, keepdims=True)
    d = y - mu
    var = jnp.mean(d * d, axis=-1, keepdims=True)
    return d * lax.rsqrt(var + LN_EPS) * g + b


def _halo_map(tm, rows=SUBLANES):
    return lambda i: (jnp.maximum(i * (tm // rows) - 1, 0), 0)


def _split3(x):
    hi = x.astype(BF16)
    r = x - hi.astype(F32)
    mid = r.astype(BF16)
    lo = (r - mid.astype(F32)).astype(BF16)
    return hi, mid, lo


def _seg_sum(x, ones_blk):
    return sum(jnp.dot(part, ones_blk, preferred_element_type=F32) for part in _split3(x))


def _dot3(a, b):
    a_hi, a_lo, _ = _split3(a)
    b_hi, b_lo, _ = _split3(b)
    dot = functools.partial(jnp.dot, preferred_element_type=F32)
    return dot(a_hi, b_hi) + (dot(a_hi, b_lo) + dot(a_lo, b_hi))


IN_WIDTHS = (RWKV_COLS, 2 * GROUP, 3 * GROUP, 4 * GROUP)
IN_COLS = sum(IN_WIDTHS)


def _in_proj_kernel(x_ref, w_ref, *outs):
    xb = x_ref[...].astype(BF16)
    col = 0
    for o in outs:
        n = o.shape[1]
        o[...] = jnp.dot(xb, w_ref[:, col:col + n], preferred_element_type=F32)
        col += n


def _in_proj(x2d, w_bf16, tm=512):
    m = x2d.shape[0]
    return pl.pallas_call(
        _in_proj_kernel,
        grid=(m // tm,),
        in_specs=[pl.BlockSpec((tm, D_MODEL), lambda i: (i, 0)),
                  pl.BlockSpec((D_MODEL, IN_COLS), lambda i: (0, 0))],
        out_specs=[pl.BlockSpec((tm, w), lambda i: (i, 0)) for w in IN_WIDTHS],
        out_shape=[jax.ShapeDtypeStruct((m, w), F32) for w in IN_WIDTHS],
        compiler_params=_cparams("arbitrary"),
        name="in_proj",
    )(x2d, w_bf16)


def _rows_to_lanes(x_s, out_ref, t0, nb, tb, gw, dup):
    steps = LANES // gw
    lane_grp = lax.broadcasted_iota(jnp.int32, (nb, LANES), 1) // gw
    z = [None] * (GROUP // gw)
    for tt in range(steps):
        xt = [x_s[c, pl.ds(t0 + tt, nb, stride=tb), :] for c in range(GROUP // LANES)]
        for g in range(len(z)):
            half = xt[g // steps]
            shift = (tt - g % steps) % steps * gw
            piece = pltpu.roll(half, shift, axis=1) if shift else half
            z[g] = piece if tt == 0 else jnp.where(lane_grp == tt, piece, z[g])
    tile = jnp.concatenate([zg for zg in z for _ in range(dup)], axis=0)
    out_ref[pl.ds(t0, steps)] = tile.T.reshape(steps, gw, tile.shape[0])


def _lanes_to_rows(y_ref, y_s, t0, nb, tb, gw):
    steps = LANES // gw
    tile = y_ref[pl.ds(t0, steps)]
    tile = tile.reshape(steps * gw, tile.shape[2]).T
    lane_grp = lax.broadcasted_iota(jnp.int32, (nb, LANES), 1) // gw
    for tt in range(steps):
        for half in range(GROUP // LANES):
            cols = None
            for pos in range(steps):
                g = half * steps + pos
                piece = tile[g * nb:(g + 1) * nb, :]
                shift = (pos - tt) % steps * gw
                piece = pltpu.roll(piece, shift, axis=1) if shift else piece
                cols = piece if pos == 0 else jnp.where(lane_grp == pos, piece, cols)
            y_s[half, pl.ds(t0 + tt, nb, stride=tb), :] = cols


def _rwkv_prep_kernel(p_ref, halo_ref, mu_ref, pv_ref, lrw_ref, ones_ref,
                      r_o, k_o, v_o, kk_o, b_o, w_o, g_o, bv_o, *stage, nb, tb):
    rows = nb * tb
    pa = p_ref[...].reshape(rows, RWKV_COLS)
    prev = jnp.where(pl.program_id(0) == 0, 0.0, halo_ref[:, SUBLANES - 1:SUBLANES, :])
    prev = jnp.broadcast_to(prev, (nb, tb, RWKV_COLS)).reshape(rows, RWKV_COLS)
    t_idx = lax.broadcasted_iota(jnp.int32, pa.shape, 0) % tb
    shifted = jnp.where(t_idx == 0, prev, pltpu.roll(pa, 1, axis=0))
    pa = pa + (shifted - pa) * mu_ref[...]
    r = pa[:, 0:GROUP]
    k = pa[:, GROUP:2 * GROUP]
    v = pa[:, 2 * GROUP:3 * GROUP]
    lr = pa[:, 3 * GROUP:RWKV_COLS]
    w0, a0, k_k, k_a, r_k = (pv_ref[n:n + 1, :] for n in range(5))
    tw = _dot3(jnp.tanh(lr), lrw_ref[0])
    ta = _dot3(lr, lrw_ref[1])
    g = _dot3(jax.nn.sigmoid(lr), lrw_ref[2])
    w_log = -_softplus(-(w0 + tw)) - 0.5
    decay = jnp.exp(-jnp.exp(w_log))
    a = jax.nn.sigmoid(a0 + ta)
    kk = k * k_k
    k2 = k * (1.0 + (a - 1.0) * k_a)
    ss = _seg_sum(kk * kk, ones_ref[...])
    kkn = kk / jnp.maximum(jnp.sqrt(ss), 1e-12)
    bonus = _seg_sum(r * k2 * r_k, ones_ref[...])
    g_o[...] = g.reshape(nb, tb, GROUP)
    bv_o[...] = (bonus * v).reshape(nb, tb, GROUP)
    layouts = ((r, r_o, HEAD_DIM, 2), (k2, k_o, HEAD_DIM, 2), (v, v_o, HEAD_DIM // 2, 1),
               (kkn, kk_o, HEAD_DIM, 2), (kkn * a, b_o, HEAD_DIM, 2), (decay, w_o, HEAD_DIM, 2))
    for (x, _, _, _), x_s in zip(layouts, stage):
        for c in range(GROUP // LANES):
            x_s[c] = x[:, c * LANES:(c + 1) * LANES]
    span = LANES // (HEAD_DIM // 2)

    def move(n, carry):
        for (_, out, gw, dup), x_s in zip(layouts, stage):
            for t0 in range(0, span, LANES // gw):
                _rows_to_lanes(x_s, out, n * span + t0, nb, tb, gw, dup)
        return carry

    lax.fori_loop(0, tb // span, move, 0)


RWKV_TB = 32


def _rwkv_prep(p_rwkv, mu, pv, lrw, ones_blk, nb, seq, tb=RWKV_TB):
    lanes = 2 * HEADS * nb
    full = lambda a: pl.BlockSpec(a.shape, lambda i: (0,) * a.ndim)
    key_spec = pl.BlockSpec((tb, HEAD_DIM, lanes), lambda i: (i, 0, 0))
    val_spec = pl.BlockSpec((tb, HEAD_DIM // 2, lanes), lambda i: (i, 0, 0))
    nat_spec = pl.BlockSpec((nb, tb, GROUP), lambda i: (0, i, 0))
    key_shape = jax.ShapeDtypeStruct((seq, HEAD_DIM, lanes), F32)
    val_shape = jax.ShapeDtypeStruct((seq, HEAD_DIM // 2, lanes), F32)
    nat_shape = jax.ShapeDtypeStruct((nb, seq, GROUP), F32)
    return pl.pallas_call(
        functools.partial(_rwkv_prep_kernel, nb=nb, tb=tb),
        grid=(seq // tb,),
        in_specs=[pl.BlockSpec((nb, tb, RWKV_COLS), lambda i: (0, i, 0)),
                  pl.BlockSpec((nb, SUBLANES, RWKV_COLS),
                               lambda i: (0, jnp.maximum(i * (tb // SUBLANES) - 1, 0), 0)),
                  full(mu), full(pv), full(lrw), full(ones_blk)],
        out_specs=[key_spec, key_spec, val_spec, key_spec, key_spec, key_spec, nat_spec, nat_spec],
        out_shape=[key_shape, key_shape, val_shape, key_shape, key_shape, key_shape, nat_shape, nat_shape],
        scratch_shapes=[pltpu.VMEM((GROUP // LANES, nb * tb, LANES), F32)] * 6,
        compiler_params=_cparams("arbitrary"),
        name="rwkv_prep",
    )(p_rwkv, p_rwkv, mu, pv, lrw, ones_blk)


def _rwkv_scan_kernel(r_ref, k_ref, v_ref, kk_ref, b_ref, w_ref, y_ref, s_ref, *, tb):
    n_j = s_ref.shape[0]
    tile = s_ref.shape[1:]

    @pl.when(pl.program_id(0) == 0)
    def _():
        s_ref[...] = jnp.zeros_like(s_ref)

    sa0 = jnp.zeros(tile, F32)
    for j in range(n_j):
        sa0 = sa0 + s_ref[j] * kk_ref[0, pl.ds(j, 1), :]

    def step(t, sa):
        t_next = jnp.minimum(t + 1, tb - 1)
        vt = v_ref[t]
        yacc = jnp.zeros(tile, F32)
        nacc = jnp.zeros(tile, F32)
        for j in range(n_j):
            sj = (s_ref[j] * w_ref[t, pl.ds(j, 1), :] - sa * b_ref[t, pl.ds(j, 1), :]
                  + vt * k_ref[t, pl.ds(j, 1), :])
            s_ref[j] = sj
            yacc = yacc + sj * r_ref[t, pl.ds(j, 1), :]
            nacc = nacc + sj * kk_ref[t_next, pl.ds(j, 1), :]
        y_ref[t] = yacc
        return nacc

    lax.fori_loop(0, tb, step, sa0)


def _rwkv_scan(r, k, v, kk, b, w, tb=64):
    seq, n_j, lanes = r.shape
    kspec = pl.BlockSpec((tb, n_j, lanes), lambda i: (i, 0, 0))
    vspec = pl.BlockSpec((tb,) + v.shape[1:], lambda i: (i, 0, 0))
    return pl.pallas_call(
        functools.partial(_rwkv_scan_kernel, tb=tb),
        grid=(seq // tb,),
        in_specs=[kspec, kspec, vspec, kspec, kspec, kspec],
        out_specs=vspec,
        out_shape=jax.ShapeDtypeStruct(v.shape, F32),
        scratch_shapes=[pltpu.VMEM((n_j,) + v.shape[1:], F32)],
        compiler_params=_cparams("arbitrary"),
        name="rwkv_scan",
    )(r, k, v, kk, b, w)


def _rwkv_post_kernel(y_ref, g_ref, bv_ref, gn_ref, ones_ref, o_ref, y_s, *, nb, tb):
    gw = y_ref.shape[1]
    steps = LANES // gw
    unroll = 2

    def move(n, carry):
        for u in range(unroll):
            _lanes_to_rows(y_ref, y_s, (n * unroll + u) * steps, nb, tb, gw)
        return carry

    lax.fori_loop(0, tb // (steps * unroll), move, 0)
    y = jnp.concatenate([y_s[c] for c in range(GROUP // LANES)], axis=1)
    inv_n = 1.0 / HEAD_DIM
    mu = _seg_sum(y, ones_ref[...]) * inv_n
    d = y - mu
    var = _seg_sum(d * d, ones_ref[...]) * inv_n
    yn = d * lax.rsqrt(var + RWKV_GN_EPS) * gn_ref[0:1, :] + gn_ref[1:2, :]
    out = (yn + bv_ref[...].reshape(nb * tb, GROUP)) * g_ref[...].reshape(nb * tb, GROUP)
    o_ref[...] = out.reshape(nb, tb, GROUP)


def _rwkv_post(y, g, bv, gn, ones_blk, nb, seq, tb=RWKV_TB):
    nat_spec = pl.BlockSpec((nb, tb, GROUP), lambda i: (0, i, 0))
    full = lambda a: pl.BlockSpec(a.shape, lambda i: (0,) * a.ndim)
    return pl.pallas_call(
        functools.partial(_rwkv_post_kernel, nb=nb, tb=tb),
        grid=(seq // tb,),
        in_specs=[pl.BlockSpec((tb,) + y.shape[1:], lambda i: (i, 0, 0)), nat_spec, nat_spec,
                  full(gn), full(ones_blk)],
        out_specs=nat_spec,
        out_shape=jax.ShapeDtypeStruct((nb, seq, GROUP), F32),
        scratch_shapes=[pltpu.VMEM((GROUP // LANES, nb * tb, LANES), F32)],
        compiler_params=_cparams("arbitrary"),
        name="rwkv_post",
    )(y, g, bv, gn, ones_blk)


def _lru_kernel(p_ref, halo_ref, cw_ref, pv_ref, wr_ref, wi_ref, o_ref, hc_ref, *, tt):
    j = pl.program_id(1)
    x = p_ref[:, 0:GROUP]
    gb = p_ref[:, GROUP:2 * GROUP]
    halo = jnp.where(j == 0, 0.0, halo_ref[:, 0:GROUP])
    conv_b, b_r, b_i, lam = (pv_ref[n:n + 1, :] for n in range(4))
    xc = x * cw_ref[3:4, :] + conv_b
    for s in (1, 2, 3):
        xc = xc + _shift_rows(x, halo, s) * cw_ref[3 - s:4 - s, :]
    gate_r = jax.nn.sigmoid(_bdot(xc, wr_ref[...]) + b_r)
    gate_i = jax.nn.sigmoid(_bdot(xc, wi_ref[...]) + b_i)
    log_a = -LRU_C * gate_r * _softplus(-lam)
    z = 2.0 * log_a
    expm1_z = jnp.tanh(0.5 * z) * (jnp.exp(z) + 1.0)
    a = jnp.exp(log_a)
    u = jnp.sqrt(-expm1_z) * (gate_i * xc)

    @pl.when(j == 0)
    def _():
        hc_ref[...] = jnp.zeros_like(hc_ref)

    row = lax.broadcasted_iota(jnp.int32, a.shape, 0)
    d = 1
    while d < tt:
        keep = row >= d
        u = a * jnp.where(keep, pltpu.roll(u, d, axis=0), 0.0) + u
        a = a * jnp.where(keep, pltpu.roll(a, d, axis=0), 1.0)
        d *= 2
    h = a * hc_ref[...] + u
    hc_ref[...] = h[tt - 1:tt, :]
    o_ref[...] = _gelu_tanh(gb) * h


def _lru(p_lru, cw, pv, wr, wi, nb, seq, tt=512):
    m = p_lru.shape[0]
    nt = seq // tt
    full = lambda a: pl.BlockSpec(a.shape, lambda b, j: (0,) * a.ndim)
    halo = _halo_map(tt)
    return pl.pallas_call(
        functools.partial(_lru_kernel, tt=tt),
        grid=(nb, nt),
        in_specs=[pl.BlockSpec((tt, 2 * GROUP), lambda b, j: (b * nt + j, 0)),
                  pl.BlockSpec((SUBLANES, 2 * GROUP), lambda b, j: halo(b * nt + j)),
                  full(cw), full(pv), full(wr), full(wi)],
        out_specs=pl.BlockSpec((tt, GROUP), lambda b, j: (b * nt + j, 0)),
        out_shape=jax.ShapeDtypeStruct((m, GROUP), F32),
        scratch_shapes=[pltpu.VMEM((1, GROUP), F32)],
        compiler_params=_cparams("arbitrary", "arbitrary"),
        name="lru",
    )(p_lru, p_lru, cw, pv, wr, wi)


def _rope_rows(x, cos, sin_signed):
    lane = lax.broadcasted_iota(jnp.int32, x.shape, 1)
    first_half = lane % HEAD_DIM < HEAD_DIM // 2
    width = x.shape[1]
    partner = jnp.where(first_half, pltpu.roll(x, width - HEAD_DIM // 2, axis=1),
                        pltpu.roll(x, HEAD_DIM // 2, axis=1))
    return x * cos + partner * sin_signed


def _rope_moba_kernel(p_ref, cos_ref, sin_ref, qt_o, k_o, vt_o, km_o):
    cos = cos_ref[...]
    sin = sin_ref[...]
    q = _rope_rows(p_ref[:, 0:GROUP], cos, sin) * HEAD_DIM ** -0.5
    qt_o[...] = q.T
    k = _rope_rows(p_ref[:, GROUP:2 * GROUP], cos, sin)
    k_o[...] = jnp.zeros_like(k_o)
    for h in range(HEADS):
        k_o[:, h * LANES:h * LANES + HEAD_DIM] = k[:, h * HEAD_DIM:(h + 1) * HEAD_DIM].astype(BF16)
    km_o[0] = jnp.mean(k, axis=0, keepdims=True)
    vt_o[...] = p_ref[:, 2 * GROUP:3 * GROUP].T.astype(BF16)


def _rope_moba(p, cos, sin, seq):
    m = p.shape[0]
    tm = MOBA_BLOCK
    nt = seq // tm
    tab = pl.BlockSpec((tm, GROUP), lambda j, b: (j, 0))
    col = pl.BlockSpec((GROUP, tm), lambda j, b: (0, b * nt + j))
    return pl.pallas_call(
        _rope_moba_kernel,
        grid=(nt, m // seq),
        in_specs=[pl.BlockSpec((tm, 3 * GROUP), lambda j, b: (b * nt + j, 0)), tab, tab],
        out_specs=[col, pl.BlockSpec((tm, HEADS * LANES), lambda j, b: (b * nt + j, 0)), col,
                   pl.BlockSpec((1, 1, GROUP), lambda j, b: (b * nt + j, 0, 0))],
        out_shape=[jax.ShapeDtypeStruct((GROUP, m), F32), jax.ShapeDtypeStruct((m, HEADS * LANES), BF16),
                   jax.ShapeDtypeStruct((GROUP, m), BF16), jax.ShapeDtypeStruct((m // tm, 1, GROUP), F32)],
        compiler_params=_cparams("arbitrary", "arbitrary"),
        name="rope_moba",
    )(p, cos, sin)


def _rope_tables(seq):
    inv = ROPE_THETA ** (-jnp.arange(0, HEAD_DIM, 2, dtype=F32) / HEAD_DIM)
    ang = jnp.arange(seq, dtype=F32)[:, None] * inv[None, :]
    cos, sin = jnp.cos(ang), jnp.sin(ang)
    cos_full = jnp.tile(jnp.concatenate([cos, cos], axis=1), (1, HEADS))
    sin_signed = jnp.tile(jnp.concatenate([-sin, sin], axis=1), (1, HEADS))
    return cos_full, sin_signed


def _moba_kernel(qt_ref, k_ref, vt_ref, km_ref, o_ref, m_s, l_s, acc_s, s_s, p_s):
    blk = pl.program_id(1)
    tq = qt_ref.shape[1]
    n_blocks = km_ref.shape[0]
    key_i = lax.broadcasted_iota(jnp.int32, (MOBA_BLOCK, tq), 0)
    qry_i = lax.broadcasted_iota(jnp.int32, (MOBA_BLOCK, tq), 1)
    blk_id = lax.broadcasted_iota(jnp.int32, (n_blocks, tq), 0)
    valid = blk_id < blk
    heads = [slice(h * HEAD_DIM, (h + 1) * HEAD_DIM) for h in range(HEADS)]

    sels = []
    for hs in heads:
        gate = _hdot(km_ref[:, hs], qt_ref[hs, :])
        sel = jnp.zeros((n_blocks, tq), F32)
        for n in range(n_blocks):
            gn = gate[n:n + 1, :]
            beats = valid & ((gate > gn) | ((gate == gn) & (blk_id < n)))
            cnt = jnp.sum(beats.astype(F32), axis=0, keepdims=True)
            sel = jnp.where((blk_id == n) & (cnt < MOBA_TOPK) & valid, 1.0, sel)
        sels.append(sel)

    def attend(start, keep_fn, first):
        for h, hs in enumerate(heads):
            qt = jnp.concatenate([qt_ref[hs, :], jnp.zeros((LANES - HEAD_DIM, tq), F32)], axis=0)
            s_s[h] = jnp.dot(k_ref[pl.ds(start, MOBA_BLOCK), h * LANES:(h + 1) * LANES], qt.astype(BF16),
                             preferred_element_type=F32)
        alphas = []
        for h in range(HEADS):
            s = jnp.where(keep_fn(h), s_s[h], NEG_INF)
            m_blk = jnp.max(s, axis=0, keepdims=True)
            if first:
                m_new = m_blk
            else:
                m_new = jnp.maximum(m_s[h], m_blk)
                alphas.append(jnp.exp(m_s[h] - m_new))
            p = jnp.exp(s - m_new)
            p_s[h] = p.astype(BF16)
            p_sum = jnp.sum(p, axis=0, keepdims=True)
            l_s[h] = p_sum if first else alphas[h] * l_s[h] + p_sum
            m_s[h] = m_new
        for h, hs in enumerate(heads):
            pv = jnp.dot(vt_ref[hs, pl.ds(start, MOBA_BLOCK)], p_s[h], preferred_element_type=F32)
            acc_s[hs, :] = pv if first else alphas[h] * acc_s[hs, :] + pv

    causal = key_i <= qry_i
    attend(pl.multiple_of(blk * MOBA_BLOCK, MOBA_BLOCK), lambda h: causal, True)

    def body(n, carry):
        picked = lambda h: jnp.sum(jnp.where(blk_id == n, sels[h], 0.0), axis=0, keepdims=True) > 0.5
        attend(pl.multiple_of(n * MOBA_BLOCK, MOBA_BLOCK), picked, False)
        return carry

    lax.fori_loop(0, blk, body, 0)
    out_t = jnp.concatenate([acc_s[hs, :] / l_s[h] for h, hs in enumerate(heads)], axis=0)
    o_ref[...] = out_t.T


def _moba(qt, kh, vt, kmean, nb, seq):
    m = qt.shape[1]
    nblk = seq // MOBA_BLOCK
    return pl.pallas_call(
        _moba_kernel,
        grid=(nb, nblk),
        in_specs=[pl.BlockSpec((GROUP, MOBA_BLOCK), lambda b, j: (0, b * nblk + j)),
                  pl.BlockSpec((seq, HEADS * LANES), lambda b, j: (b, 0)),
                  pl.BlockSpec((GROUP, seq), lambda b, j: (0, b)),
                  pl.BlockSpec((None, nblk, GROUP), lambda b, j: (b, 0, 0))],
        out_specs=pl.BlockSpec((MOBA_BLOCK, GROUP), lambda b, j: (b * nblk + j, 0)),
        out_shape=jax.ShapeDtypeStruct((m, GROUP), F32),
        scratch_shapes=[pltpu.VMEM((HEADS, 1, MOBA_BLOCK), F32), pltpu.VMEM((HEADS, 1, MOBA_BLOCK), F32),
                        pltpu.VMEM((GROUP, MOBA_BLOCK), F32),
                        pltpu.VMEM((HEADS, MOBA_BLOCK, MOBA_BLOCK), F32),
                        pltpu.VMEM((HEADS, MOBA_BLOCK, MOBA_BLOCK), BF16)],
        compiler_params=_cparams("arbitrary", "arbitrary"),
        name="moba",
    )(qt, kh, vt, kmean.reshape(nb, nblk, GROUP))


def _ret_kernel(p_ref, cos_ref, sin_ref, gn_ref, o_ref, st_ref, mask_s):
    lc = p_ref.shape[0]
    log_gammas = [math.log1p(-(2.0 ** (-5.0 - h))) for h in range(HEADS)]

    @pl.when(pl.program_id(1) == 0)
    def _():
        st_ref[...] = jnp.zeros_like(st_ref)

    @pl.when((pl.program_id(0) == 0) & (pl.program_id(1) == 0))
    def _():
        ri = lax.broadcasted_iota(jnp.int32, (lc, lc), 0)
        ci = lax.broadcasted_iota(jnp.int32, (lc, lc), 1)
        diff = (ri - ci).astype(F32)
        for h in range(HEADS):
            mask_s[h] = jnp.where(diff >= 0, jnp.exp(log_gammas[h] * jnp.maximum(diff, 0.0)), 0.0)

    cos = cos_ref[...]
    sin = sin_ref[...]
    q_all = _rope_rows(p_ref[:, 0:GROUP], cos, sin)
    k_all = _rope_rows(p_ref[:, GROUP:2 * GROUP], cos, sin) * HEAD_DIM ** -0.5
    pos = lax.broadcasted_iota(jnp.int32, (lc, 1), 0).astype(F32)
    for h in range(HEADS):
        hs = slice(h * HEAD_DIM, (h + 1) * HEAD_DIM)
        log_gamma = log_gammas[h]
        q = q_all[:, hs]
        k = k_all[:, hs]
        v = p_ref[:, 2 * GROUP + h * HEAD_DIM:2 * GROUP + (h + 1) * HEAD_DIM]
        g = p_ref[:, 3 * GROUP + h * HEAD_DIM:3 * GROUP + (h + 1) * HEAD_DIM]
        intra = _bdot_t(q, k) * mask_s[h]
        state = st_ref[h]
        out = _bdot(intra, v) + _bdot(q * jnp.exp(log_gamma * (pos + 1.0)), state)
        k_w = jnp.exp(log_gamma * (lc - 1.0 - pos))
        kv = lax.dot_general((k * k_w).astype(BF16), v.astype(BF16), (((0,), (0,)), ((), ())),
                             preferred_element_type=F32)
        st_ref[h] = state * math.exp(log_gamma * lc) + kv
        mu = jnp.mean(out, axis=-1, keepdims=True)
        d = out - mu
        var = jnp.mean(d * d, axis=-1, keepdims=True)
        yn = d * lax.rsqrt(var + GN_EPS) * gn_ref[0:1, hs] + gn_ref[1:2, hs]
        o_ref[:, hs] = g * jax.nn.sigmoid(g) * yn


def _retention(p_ret, cos, sin, gn, nb, seq):
    m = p_ret.shape[0]
    lc = RET_CHUNK
    nt = seq // lc
    tab = pl.BlockSpec((lc, GROUP), lambda b, j: (j, 0))
    return pl.pallas_call(
        _ret_kernel,
        grid=(nb, nt),
        in_specs=[pl.BlockSpec((lc, 4 * GROUP), lambda b, j: (b * nt + j, 0)), tab, tab,
                  pl.BlockSpec(gn.shape, lambda b, j: (0, 0))],
        out_specs=pl.BlockSpec((lc, GROUP), lambda b, j: (b * nt + j, 0)),
        out_shape=jax.ShapeDtypeStruct((m, GROUP), F32),
        scratch_shapes=[pltpu.VMEM((HEADS, HEAD_DIM, HEAD_DIM), F32), pltpu.VMEM((HEADS, lc, lc), F32)],
        compiler_params=_cparams("arbitrary", "arbitrary"),
        name="retention",
    )(p_ret, cos, sin, gn)


def _out_proj_kernel(ya_ref, yb_ref, yc_ref, yd_ref, x_ref, w_ref, ln_ref, o_ref, *, alpha):
    y = jnp.concatenate([r[...].astype(BF16) for r in (ya_ref, yb_ref, yc_ref, yd_ref)], axis=1)
    acc = alpha * x_ref[...] + jnp.dot(y, w_ref[...], preferred_element_type=F32)
    o_ref[...] = _layer_norm(acc, ln_ref[0:1, :], ln_ref[1:2, :])


def _out_proj(ys, x2d, w_bf16, ln, alpha, tm=1024):
    m = x2d.shape[0]
    yspec = pl.BlockSpec((tm, GROUP), lambda i: (i, 0))
    xspec = pl.BlockSpec((tm, D_MODEL), lambda i: (i, 0))
    full = lambda a: pl.BlockSpec(a.shape, lambda i: (0,) * a.ndim)
    return pl.pallas_call(
        functools.partial(_out_proj_kernel, alpha=alpha),
        grid=(m // tm,),
        in_specs=[yspec] * 4 + [xspec, full(w_bf16), full(ln)],
        out_specs=xspec,
        out_shape=jax.ShapeDtypeStruct((m, D_MODEL), F32),
        compiler_params=_cparams("arbitrary"),
        name="out_proj_ln",
    )(*ys, x2d, w_bf16, ln)


FFN_TN = 256


def _ffn_up_kernel(x_ref, w_ref, cw_ref, cb_ref, o_ref, tail_ref, h_s, *, tiles_per_seq):
    xb = x_ref[...].astype(BF16)
    tm = xb.shape[0]
    first = pl.program_id(0) % tiles_per_seq == 0

    @pl.when(pl.program_id(0) == 0)
    def _():
        tail_ref[...] = jnp.zeros_like(tail_ref)

    def conv_cols(c0, slot):
        h = jnp.dot(xb, w_ref[:, c0:c0 + FFN_TN], preferred_element_type=F32)
        h_s[slot, 0:SUBLANES, :] = jnp.where(first, 0.0, tail_ref[:, c0:c0 + FFN_TN])
        h_s[slot, SUBLANES:, :] = h
        tail_ref[:, c0:c0 + FFN_TN] = h[tm - SUBLANES:tm]
        u = h * cw_ref[2:3, c0:c0 + FFN_TN] + cb_ref[:, c0:c0 + FFN_TN]
        for s in (1, 2):
            u = u + h_s[slot, pl.ds(SUBLANES - s, tm), :] * cw_ref[2 - s:3 - s, c0:c0 + FFN_TN]
        return u

    for c in range(D_FF // FFN_TN):
        gate = conv_cols(c * FFN_TN, 0)
        val = conv_cols(D_FF + c * FFN_TN, 1)
        o_ref[:, c * FFN_TN:(c + 1) * FFN_TN] = (_gelu_tanh(gate) * val).astype(BF16)


def _ffn_up(x2d, w_bf16, cw, cb, seq, tm=512):
    m = x2d.shape[0]
    full = lambda a: pl.BlockSpec(a.shape, lambda i: (0,) * a.ndim)
    return pl.pallas_call(
        functools.partial(_ffn_up_kernel, tiles_per_seq=seq // tm),
        grid=(m // tm,),
        in_specs=[pl.BlockSpec((tm, D_MODEL), lambda i: (i, 0)), full(w_bf16), full(cw), full(cb)],
        out_specs=pl.BlockSpec((tm, D_FF), lambda i: (i, 0)),
        out_shape=jax.ShapeDtypeStruct((m, D_FF), BF16),
        scratch_shapes=[pltpu.VMEM((SUBLANES, 2 * D_FF), F32), pltpu.VMEM((2, SUBLANES + tm, FFN_TN), F32)],
        compiler_params=_cparams("arbitrary"),
        name="ffn_up",
    )(x2d, w_bf16, cw, cb)


def _ffn_down_kernel(a_ref, x_ref, w_ref, ln_ref, o_ref, *, alpha):
    acc = alpha * x_ref[...] + jnp.dot(a_ref[...], w_ref[...], preferred_element_type=F32)
    o_ref[...] = _layer_norm(acc, ln_ref[0:1, :], ln_ref[1:2, :])


def _ffn_down(act, x2d, w_bf16, ln, alpha, tm=1024):
    m = x2d.shape[0]
    xspec = pl.BlockSpec((tm, D_MODEL), lambda i: (i, 0))
    full = lambda a: pl.BlockSpec(a.shape, lambda i: (0,) * a.ndim)
    return pl.pallas_call(
        functools.partial(_ffn_down_kernel, alpha=alpha),
        grid=(m // tm,),
        in_specs=[pl.BlockSpec((tm, D_FF), lambda i: (i, 0)), xspec, full(w_bf16), full(ln)],
        out_specs=xspec,
        out_shape=jax.ShapeDtypeStruct((m, D_MODEL), F32),
        compiler_params=_cparams("arbitrary"),
        name="ffn_down_ln",
    )(act, x2d, w_bf16, ln)


def _block_diag(w):
    h, n, _ = w.shape
    eye = jnp.eye(h, dtype=w.dtype)
    return (eye[:, None, :, None] * w[:, :, None, :]).reshape(h * n, h * n)


def _pad_rows(w, offset, total):
    return jnp.zeros((total, w.shape[1]), w.dtype).at[offset:offset + w.shape[0]].set(w)


def _rows8(*vecs):
    rows = [v.reshape(1, -1) for v in vecs]
    rows += [jnp.zeros_like(rows[0])] * (SUBLANES - len(rows))
    return jnp.concatenate(rows, axis=0)


def kernel(x, w_in, tshift_mu, rwkv_w0, rwkv_w_up, rwkv_a0, rwkv_a_up, rwkv_g_up, rwkv_k_k, rwkv_k_a, rwkv_r_k, rwkv_gn_g, rwkv_gn_b, lru_conv_w, lru_conv_b, lru_w_r, lru_b_r, lru_w_i, lru_b_i, lru_lambda, ret_gn_g, ret_gn_b, w_out, ln1_g, ln1_b, ffn_w_up, ffn_conv_w, ffn_conv_b, ffn_w_down, ln2_g, ln2_b):
    nb, seq, _ = x.shape
    depth = w_in.shape[0]
    alpha = (2.0 * depth) ** 0.25
    cos, sin = _rope_tables(seq)
    ones_blk = _block_diag(jnp.ones((HEADS, HEAD_DIM, HEAD_DIM), F32)).astype(BF16)
    x2d = x.reshape(nb * seq, D_MODEL)
    for l in range(depth):
        p_rwkv, p_lru, p_att, p_ret = _in_proj(x2d, w_in[l].astype(BF16))

        lrw = jnp.stack([_pad_rows(rwkv_w_up[l], 0, RWKV_LR), _pad_rows(rwkv_a_up[l], 32, RWKV_LR),
                         _pad_rows(rwkv_g_up[l], 64, RWKV_LR)])
        pv = _rows8(rwkv_w0[l], rwkv_a0[l], rwkv_k_k[l], rwkv_k_a[l], rwkv_r_k[l])
        *scan_ops, g, bv = _rwkv_prep(p_rwkv.reshape(nb, seq, RWKV_COLS), tshift_mu[l].reshape(1, -1), pv, lrw,
                                      ones_blk, nb, seq)
        y = _rwkv_scan(*scan_ops)
        y_a = _rwkv_post(y, g, bv, _rows8(rwkv_gn_g[l], rwkv_gn_b[l]), ones_blk, nb, seq)
        y_a = y_a.reshape(nb * seq, GROUP)

        y_b = _lru(p_lru, lru_conv_w[l], _rows8(lru_conv_b[l], lru_b_r[l], lru_b_i[l], lru_lambda[l]),
                   _block_diag(lru_w_r[l]).astype(BF16), _block_diag(lru_w_i[l]).astype(BF16), nb, seq)

        y_c = _moba(*_rope_moba(p_att, cos, sin, seq), nb, seq)

        y_d = _retention(p_ret, cos, sin, _rows8(ret_gn_g[l], ret_gn_b[l]), nb, seq)

        x2d = _out_proj((y_a, y_b, y_c, y_d), x2d, w_out[l].astype(BF16), _rows8(ln1_g[l], ln1_b[l]), alpha)
        act = _ffn_up(x2d, ffn_w_up[l].astype(BF16), ffn_conv_w[l], ffn_conv_b[l].reshape(1, -1), seq)
        x2d = _ffn_down(act, x2d, ffn_w_down[l].astype(BF16), _rows8(ln2_g[l], ln2_b[l]), alpha)
    return x2d.reshape(nb, seq, D_MODEL)
```
